```python
import jax, jax.numpy as jnp
from jax import lax
import numpy as np


D_MODEL = 2048
BATCH = 1
SEQ = 8192
DEPTH = 4

GRID_W = 64
CTX_LEN = 256
EPS = 1e-6
ROPE_THETA = 10000.0

HEAD_DIM = 128
ATTN_HEADS = 8
ATTN_KV_HEADS = 2
ATTN_WIDTH = ATTN_HEADS * HEAD_DIM
KV_WIDTH = ATTN_KV_HEADS * HEAD_DIM
Q_BLOCK = 128

GLA_HEADS = 4
GLA_VALUE_WIDTH = D_MODEL - ATTN_WIDTH
GLA_KEY_WIDTH = GLA_VALUE_WIDTH // 2
GLA_DK = GLA_KEY_WIDTH // GLA_HEADS
GLA_DV = GLA_VALUE_WIDTH // GLA_HEADS
GLA_GATE_RANK = 16
GLA_GATE_TEMP = 16.0
GLA_CHUNK = 64

MIX_WIDTH = ATTN_WIDTH + GLA_VALUE_WIDTH
IN_WIDTH = ATTN_WIDTH + 2 * KV_WIDTH + 2 * GLA_KEY_WIDTH + 2 * GLA_VALUE_WIDTH + 2 * GLA_GATE_RANK

N_EXPERTS = 16
EXPERT_FF = 3 * D_MODEL // 4
CAPACITY_FACTOR = 2

kernel_name = 'hybrid_gqa_gla_ecmoe_diffusion_trunk'


def rmsnorm(x, g):
    x32 = x.astype(jnp.float32)
    y = x32 * lax.rsqrt(jnp.mean(x32 * x32, axis=-1, keepdims=True) + EPS)
    return (y * g.astype(jnp.float32)).astype(x.dtype)


def modulation(cvec, w_mod, b_mod):
    m = jax.nn.silu(cvec) @ w_mod + b_mod
    return jnp.split(m[:, None, :], 6, axis=-1)


def modulate(x, g, shift, scale):
    return rmsnorm(x, g) * (1.0 + scale) + shift


def axial_rope(rows, dtype):
    row = jnp.repeat(jnp.arange(rows, dtype=jnp.float32), GRID_W)
    col = jnp.tile(jnp.arange(GRID_W, dtype=jnp.float32), rows)
    half = HEAD_DIM // 2
    inv_freq = ROPE_THETA ** (-jnp.arange(0, half, 2, dtype=jnp.float32) / half)
    ang = jnp.concatenate([row[:, None] * inv_freq, col[:, None] * inv_freq], axis=-1)
    return jnp.cos(ang).astype(dtype), jnp.sin(ang).astype(dtype)


def apply_rope(x, cos, sin):
    x2 = x.reshape(x.shape[:-1] + (HEAD_DIM // 2, 2))
    x0, x1 = x2[..., 0], x2[..., 1]
    c = cos[None, :, None, :].astype(x.dtype)
    s = sin[None, :, None, :].astype(x.dtype)
    return jnp.stack([x0 * c - x1 * s, x0 * s + x1 * c], axis=-1).reshape(x.shape)


def mixer_inputs(h, w_in, q_gain, k_gain, w_a2, b_a, rope):
    B, L, _ = h.shape
    widths = [ATTN_WIDTH, KV_WIDTH, KV_WIDTH, GLA_KEY_WIDTH, GLA_KEY_WIDTH,
              GLA_VALUE_WIDTH, GLA_VALUE_WIDTH, GLA_GATE_RANK, GLA_GATE_RANK]
    offsets = np.cumsum(widths)[:-1].tolist()
    p = h @ w_in
    q, k, v, gq, gk, gv, r, af, ab = jnp.split(p, offsets, axis=-1)
    q = rmsnorm(q.reshape(B, L, ATTN_HEADS, HEAD_DIM), q_gain)
    k = rmsnorm(k.reshape(B, L, ATTN_KV_HEADS, HEAD_DIM), k_gain)
    v = v.reshape(B, L, ATTN_KV_HEADS, HEAD_DIM)
    if rope is not None:
        q = apply_rope(q, *rope)
        k = apply_rope(k, *rope)
    gq = gq.reshape(B, L, GLA_HEADS, GLA_DK) * (GLA_DK ** -0.5)
    gk = gk.reshape(B, L, GLA_HEADS, GLA_DK)
    gv = gv.reshape(B, L, GLA_HEADS, GLA_DV)
    la_f = jax.nn.log_sigmoid((af @ w_a2[0] + b_a[0]).astype(jnp.float32)) / GLA_GATE_TEMP
    la_b = jax.nn.log_sigmoid((ab @ w_a2[1] + b_a[1]).astype(jnp.float32)) / GLA_GATE_TEMP
    la_f = la_f.reshape(B, L, GLA_HEADS, GLA_DK)
    la_b = la_b.reshape(B, L, GLA_HEADS, GLA_DK)
    return (q, k, v), (gq, gk, gv, la_f, la_b), r


def softmax_attend(qb, k, v):
    s = jnp.einsum('bqkgd,bskd->bkgqs', qb, k, preferred_element_type=jnp.float32) * (HEAD_DIM ** -0.5)
    p = jax.nn.softmax(s, axis=-1).astype(v.dtype)
    return jnp.einsum('bkgqs,bskd->bqkgd', p, v)


def attention_group(lat, ctx, with_ctx_out):
    q, k, v = lat
    cq, ck, cv = ctx
    B, T = q.shape[:2]
    G = ATTN_HEADS // ATTN_KV_HEADS
    k_all = jnp.concatenate([ck, k], axis=1)
    v_all = jnp.concatenate([cv, v], axis=1)
    qb = q.reshape(B, T // Q_BLOCK, Q_BLOCK, ATTN_KV_HEADS, G, HEAD_DIM).swapaxes(0, 1)
    o = lax.map(lambda blk: softmax_attend(blk, k_all, v_all), qb)
    o_lat = o.swapaxes(0, 1).reshape(B, T, ATTN_WIDTH)
    o_ctx = None
    if with_ctx_out:
        C = cq.shape[1]
        o_ctx = softmax_attend(cq.reshape(B, C, ATTN_KV_HEADS, G, HEAD_DIM), ck, cv).reshape(B, C, ATTN_WIDTH)
    return o_lat, o_ctx


def gla_scan(q, k, v, log_a, s0):
    B, L, H, dk = q.shape
    dv = v.shape[-1]
    nc = L // GLA_CHUNK

    def to_chunks(a):
        return a.reshape(B, nc, GLA_CHUNK, H, a.shape[-1]).transpose(1, 0, 3, 2, 4)

    qc, kc, vc, ac = to_chunks(q), to_chunks(k), to_chunks(v), to_chunks(log_a)
    mask = jnp.tril(jnp.ones((GLA_CHUNK, GLA_CHUNK), dtype=bool))

    def step(S, inp):
        qi, ki, vi, ai = inp
        qi = qi.astype(jnp.float32)
        ki = ki.astype(jnp.float32)
        vi = vi.astype(jnp.float32)
        b = jnp.cumsum(ai, axis=2)
        diff = b[:, :, :, None, :] - b[:, :, None, :, :]
        decay = jnp.exp(jnp.where(mask[:, :, None], diff, -jnp.inf))
        A = jnp.einsum('bhid,bhjd,bhijd->bhij', qi, ki, decay)
        o = jnp.einsum('bhij,bhjv->bhiv', A, vi) + jnp.einsum('bhid,bhdv->bhiv', qi * jnp.exp(b), S)
        b_last = b[:, :, -1, :]
        S_new = jnp.exp(b_last)[..., None] * S + jnp.einsum(
            'bhjd,bhjv->bhdv', ki * jnp.exp(b_last[:, :, None, :] - b), vi)
        return S_new, o

    S_final, o = lax.scan(step, s0, (qc, kc, vc, ac))
    o = o.transpose(1, 0, 3, 2, 4).reshape(B, L, H, dv).astype(v.dtype)
    return o, S_final


def bidirectional_gla(lat, ctx):
    gq, gk, gv, la_f, la_b = lat
    cq, ck, cv, cla_f, cla_b = ctx
    B = gq.shape[0]
    s0 = jnp.zeros((B, GLA_HEADS, GLA_DK, GLA_DV), jnp.float32)
    flip = lambda a: jnp.flip(a, axis=1)
    o_cf, s_cf = gla_scan(cq, ck, cv, cla_f, s0)
    o_lf, _ = gla_scan(gq, gk, gv, la_f, s_cf)
    o_cb, s_cb = gla_scan(flip(cq), flip(ck), flip(cv), flip(cla_b), s0)
    o_lb, _ = gla_scan(flip(gq), flip(gk), flip(gv), flip(la_b), s_cb)
    return o_lf + flip(o_lb), o_cf + flip(o_cb)


def mixer_output(attn_o, gla_o, r, gla_gain, w_out):
    B, L = attn_o.shape[:2]
    gla = rmsnorm(gla_o, gla_gain).reshape(B, L, GLA_VALUE_WIDTH) * jax.nn.silu(r)
    return jnp.concatenate([attn_o, gla], axis=-1) @ w_out


def expert_choice_ffn(h, w_router, w_gate, w_up, w_down):
    B, L, D = h.shape
    cap = CAPACITY_FACTOR * L // N_EXPERTS
    aff = jax.nn.softmax((h @ w_router).astype(jnp.float32), axis=-1)
    gate_vals, idx = lax.top_k(jnp.swapaxes(aff, 1, 2), cap)
    flat_idx = idx.reshape(B, N_EXPERTS * cap)
    xg = jnp.take_along_axis(h, flat_idx[..., None], axis=1).reshape(B, N_EXPERTS, cap, D)
    hid = jax.nn.silu(jnp.einsum('becd,edf->becf', xg, w_gate)) * jnp.einsum('becd,edf->becf', xg, w_up)
    out = jnp.einsum('becf,efd->becd', hid, w_down) * gate_vals[..., None].astype(h.dtype)
    return jax.vmap(lambda i, o: jnp.zeros((L, D), h.dtype).at[i].add(o))(
        flat_idx, out.reshape(B, N_EXPERTS * cap, D))


def setup_inputs(seed: int = 0) -> dict:
    key = jax.random.key(seed)
    ks = jax.random.split(key, 20)
    f32 = jnp.float32

    def nrm(k, shape, scale):
        return jax.random.normal(k, shape, f32) * scale

    def gain(k, shape):
        return 1.0 + 0.02 * jax.random.normal(k, shape, f32)

    return {
        'x': nrm(ks[0], (BATCH, SEQ, D_MODEL), 1.0),
        'c': nrm(ks[1], (BATCH, D_MODEL), 1.0),
        'ctx': nrm(ks[2], (BATCH, CTX_LEN, D_MODEL), 1.0),
        'c_ctx': nrm(ks[3], (D_MODEL,), 1.0),
        'w_mod': nrm(ks[4], (DEPTH, D_MODEL, 6 * D_MODEL), 0.5 * D_MODEL ** -0.5),
        'b_mod': nrm(ks[5], (DEPTH, 6 * D_MODEL), 0.02),
        'norm_mix': gain(ks[6], (DEPTH, D_MODEL)),
        'w_in': nrm(ks[7], (DEPTH, D_MODEL, IN_WIDTH), D_MODEL ** -0.5),
        'q_gain': gain(ks[8], (DEPTH, HEAD_DIM)),
        'k_gain': gain(ks[9], (DEPTH, HEAD_DIM)),
        'w_gla_a2': nrm(ks[10], (DEPTH, 2, GLA_GATE_RANK, GLA_KEY_WIDTH), GLA_GATE_RANK ** -0.5),
        'b_gla_a': nrm(ks[11], (DEPTH, 2, GLA_KEY_WIDTH), 0.1),
        'gla_gain': gain(ks[12], (DEPTH, GLA_DV)),
        'w_out': nrm(ks[13], (DEPTH, MIX_WIDTH, D_MODEL), MIX_WIDTH ** -0.5),
        'norm_ffn': gain(ks[14], (DEPTH, D_MODEL)),
        'w_router': nrm(ks[15], (DEPTH, D_MODEL, N_EXPERTS), D_MODEL ** -0.5),
        'w_gate': nrm(ks[16], (DEPTH, N_EXPERTS, D_MODEL, EXPERT_FF), D_MODEL ** -0.5),
        'w_up': nrm(ks[17], (DEPTH, N_EXPERTS, D_MODEL, EXPERT_FF), D_MODEL ** -0.5),
        'w_down': nrm(ks[18], (DEPTH, N_EXPERTS, EXPERT_FF, D_MODEL), EXPERT_FF ** -0.5),
        'final_norm': gain(ks[19], (D_MODEL,)),
    }


def reference(x, c, ctx, c_ctx, w_mod, b_mod, norm_mix, w_in, q_gain, k_gain, w_gla_a2, b_gla_a,
              gla_gain, w_out, norm_ffn, w_router, w_gate, w_up, w_down, final_norm):
    T = x.shape[1]
    rows = T // GRID_W
    rope = axial_rope(rows, x.dtype)
    xc = ctx
    for l in range(DEPTH):
        last = l == DEPTH - 1
        sh1, sc1, g1, sh2, sc2, g2 = modulation(c, w_mod[l], b_mod[l])
        csh1, csc1, cg1, csh2, csc2, cg2 = modulation(c_ctx[None, :], w_mod[l], b_mod[l])
        lat_attn, lat_gla, lat_r = mixer_inputs(modulate(x, norm_mix[l], sh1, sc1), w_in[l], q_gain[l],
                                                k_gain[l], w_gla_a2[l], b_gla_a[l], rope)
        ctx_attn, ctx_gla, ctx_r = mixer_inputs(modulate(xc, norm_mix[l], csh1, csc1), w_in[l], q_gain[l],
                                                k_gain[l], w_gla_a2[l], b_gla_a[l], None)
        a_lat, a_ctx = attention_group(lat_attn, ctx_attn, not last)
        o_lat, o_ctx = bidirectional_gla(lat_gla, ctx_gla)
        x = x + g1 * mixer_output(a_lat, o_lat, lat_r, gla_gain[l], w_out[l])
        x = x + g2 * expert_choice_ffn(modulate(x, norm_ffn[l], sh2, sc2), w_router[l], w_gate[l],
                                       w_up[l], w_down[l])
        if not last:
            xc = xc + cg1 * mixer_output(a_ctx, o_ctx, ctx_r, gla_gain[l], w_out[l])
            xc = xc + cg2 * expert_choice_ffn(modulate(xc, norm_ffn[l], csh2, csc2), w_router[l],
                                              w_gate[l], w_up[l], w_down[l])
    return rmsnorm(x, final_norm)
```

```python
import functools
import math

import jax
import jax.numpy as jnp
from jax import lax
from jax.experimental import pallas as pl
from jax.experimental.pallas import tpu as pltpu

D_MODEL = 2048
HEAD_DIM = 128
ATTN_HEADS = 8
ATTN_KV_HEADS = 2
ATTN_GROUP = ATTN_HEADS // ATTN_KV_HEADS
ATTN_WIDTH = ATTN_HEADS * HEAD_DIM
KV_WIDTH = ATTN_KV_HEADS * HEAD_DIM
GLA_HEADS = 4
GLA_DK = 128
GLA_DV = 256
GLA_KEY_WIDTH = GLA_HEADS * GLA_DK
GLA_VALUE_WIDTH = GLA_HEADS * GLA_DV
GLA_GATE_RANK = 16
GLA_GATE_TEMP = 16.0
GLA_CHUNK = 64
GLA_SUB = 16
N_EXPERTS = 16
EXPERT_FF = 1536
CAPACITY_FACTOR = 2
GRID_W = 64
ROPE_THETA = 10000.0
EPS = 1e-6
IN_WIDTH = ATTN_WIDTH + 2 * KV_WIDTH + 2 * GLA_KEY_WIDTH + 2 * GLA_VALUE_WIDTH + 2 * GLA_GATE_RANK

LANE = 128
SUBLANE = 8
SLABS = D_MODEL // LANE
VMEM_LIMIT_BYTES = 56 * 1024 * 1024

IN_WIDTH_PAD = ((IN_WIDTH + LANE - 1) // LANE) * LANE
GATE_COL = IN_WIDTH_PAD - LANE

F32 = jnp.float32
BF16 = jnp.bfloat16


def _cparams(sem):
    return pltpu.CompilerParams(dimension_semantics=sem, vmem_limit_bytes=VMEM_LIMIT_BYTES)


def _dot(a, b):
    return jnp.dot(a, b, preferred_element_type=F32)


def _dot_nt(a, b):
    return lax.dot_general(a, b, (((1,), (1,)), ((), ())), preferred_element_type=F32)


def _dot_tn(a, b):
    return lax.dot_general(a, b, (((0,), (0,)), ((), ())), preferred_element_type=F32)


def _split2(a):
    hi = a.astype(BF16)
    lo = (a - hi.astype(F32)).astype(BF16)
    return hi, lo


def _split3(a):
    hi = a.astype(BF16)
    r1 = a - hi.astype(F32)
    mid = r1.astype(BF16)
    lo = (r1 - mid.astype(F32)).astype(BF16)
    return hi, mid, lo


def _dot_split(a, w):
    a_hi, a_lo = _split2(a)
    w_hi, w_lo = _split2(w)
    return _dot(a_hi, w_hi) + _dot(a_lo, w_hi) + _dot(a_hi, w_lo)


def _sigmoid(x):
    return 1.0 / (1.0 + jnp.exp(-x))


def _silu(x):
    return x * _sigmoid(x)


def _rms(x, gain):
    return x * lax.rsqrt(jnp.mean(x * x, axis=-1, keepdims=True) + EPS) * gain


def _load_rows(ref):
    return jnp.concatenate([ref[:, j, :] for j in range(SLABS)], axis=1)


def _store_rows(ref, val):
    for j in range(SLABS):
        ref[:, j, :] = val[:, j * LANE:(j + 1) * LANE]


def _mod_kernel(c_ref, w_ref, b_ref, o_ref):
    o_ref[0] = _dot_split(_silu(c_ref[...]), w_ref[0]) + b_ref[0]


def _modulation(cvec, w_mod, b_mod):
    depth, _, width = w_mod.shape
    tile = 1536
    return pl.pallas_call(
        _mod_kernel,
        grid=(depth, width // tile),
        in_specs=[
            pl.BlockSpec((SUBLANE, D_MODEL), lambda l, j: (0, 0)),
            pl.BlockSpec((1, D_MODEL, tile), lambda l, j: (l, 0, j)),
            pl.BlockSpec((1, 1, tile), lambda l, j: (l, 0, j)),
        ],
        out_specs=pl.BlockSpec((1, SUBLANE, tile), lambda l, j: (l, 0, j)),
        out_shape=jax.ShapeDtypeStruct((depth, SUBLANE, width), F32),
        compiler_params=_cparams(("arbitrary", "arbitrary")),
        name="modulation",
    )(cvec, w_mod, b_mod[:, None, :])


def _mod_rows(m_ref, is_ctx, k):
    lo, hi = k * D_MODEL, (k + 1) * D_MODEL
    return jnp.where(is_ctx, m_ref[1:2, lo:hi], m_ref[0:1, lo:hi])


def _rope(x, cos, sin_signed, even):
    partner = jnp.where(even, pltpu.roll(x, LANE - 1, 1), pltpu.roll(x, 1, 1))
    return x * cos + partner * sin_signed


def _inproj_kernel(n_ctx_tiles, x_ref, m_ref, g_ref, w_ref, qg_ref, kg_ref, w2_ref, b2_ref, cos_ref, sin_ref,
                   q_ref, k_ref, v_ref, gq_ref, gk_ref, gv_ref, r_ref, la_ref):
    is_ctx = pl.program_id(0) < n_ctx_tiles
    x = _load_rows(x_ref)
    h = _rms(x, g_ref[...]) * (1.0 + _mod_rows(m_ref, is_ctx, 1)) + _mod_rows(m_ref, is_ctx, 0)
    hb = h.astype(BF16)
    cos = cos_ref[...]
    sin = sin_ref[...]
    even = (lax.broadcasted_iota(jnp.int32, cos.shape, 1) % 2) == 0
    q_scale = HEAD_DIM ** -0.5 * math.log2(math.e)

    def head(p, gain, scale):
        y = _rope(_rms(p, gain), cos, sin, even)
        return (y * scale).astype(BF16) if scale != 1.0 else y.astype(BF16)

    pq = _dot(hb, w_ref[:, 0:ATTN_WIDTH])
    for hh in range(ATTN_HEADS):
        sl = slice(hh * HEAD_DIM, (hh + 1) * HEAD_DIM)
        q_ref[:, sl] = head(pq[:, sl], qg_ref[...], q_scale)
    c0 = ATTN_WIDTH
    pk = _dot(hb, w_ref[:, c0:c0 + KV_WIDTH])
    for hh in range(ATTN_KV_HEADS):
        sl = slice(hh * HEAD_DIM, (hh + 1) * HEAD_DIM)
        k_ref[:, sl] = head(pk[:, sl], kg_ref[...], 1.0)
    c0 += KV_WIDTH
    v_ref[...] = _dot(hb, w_ref[:, c0:c0 + KV_WIDTH]).astype(BF16)
    c0 += KV_WIDTH
    gq_ref[...] = _dot(hb, w_ref[:, c0:c0 + GLA_KEY_WIDTH]) * (GLA_DK ** -0.5)
    c0 += GLA_KEY_WIDTH
    gk_ref[...] = _dot(hb, w_ref[:, c0:c0 + GLA_KEY_WIDTH])
    c0 += GLA_KEY_WIDTH
    gv_ref[...] = _dot(hb, w_ref[:, c0:c0 + GLA_VALUE_WIDTH]).astype(BF16)
    c0 += GLA_VALUE_WIDTH
    r_ref[...] = _dot(hb, w_ref[:, c0:c0 + GLA_VALUE_WIDTH])
    a = _dot(hb, w_ref[:, GATE_COL:GATE_COL + LANE])
    z = _dot_split(a, w2_ref[...]) + b2_ref[...]
    la_ref[...] = (jnp.minimum(z, 0.0) - jnp.log(1.0 + jnp.exp(-jnp.abs(z)))) * (1.0 / GLA_GATE_TEMP)


def _inproj(x3, mods_l, norm_g, w_in_b, q_gain, k_gain, w2, b2, cos_t, sin_t, tm, n_ctx_tiles):
    rows = x3.shape[0]
    row = lambda i: (i, 0)
    const = lambda i: (0, 0)
    out_shapes = (
        jax.ShapeDtypeStruct((rows, ATTN_WIDTH), BF16),
        jax.ShapeDtypeStruct((rows, KV_WIDTH), BF16),
        jax.ShapeDtypeStruct((rows, KV_WIDTH), BF16),
        jax.ShapeDtypeStruct((rows, GLA_KEY_WIDTH), F32),
        jax.ShapeDtypeStruct((rows, GLA_KEY_WIDTH), F32),
        jax.ShapeDtypeStruct((rows, GLA_VALUE_WIDTH), BF16),
        jax.ShapeDtypeStruct((rows, GLA_VALUE_WIDTH), F32),
        jax.ShapeDtypeStruct((rows, 2 * GLA_KEY_WIDTH), F32),
    )
    return pl.pallas_call(
        functools.partial(_inproj_kernel, n_ctx_tiles),
        grid=(rows // tm,),
        in_specs=[
            pl.BlockSpec((tm, SLABS, LANE), lambda i: (i, 0, 0)),
            pl.BlockSpec(mods_l.shape, const),
            pl.BlockSpec((1, D_MODEL), const),
            pl.BlockSpec(w_in_b.shape, const, pipeline_mode=pl.Buffered(1)),
            pl.BlockSpec((1, HEAD_DIM), const),
            pl.BlockSpec((1, HEAD_DIM), const),
            pl.BlockSpec(w2.shape, const),
            pl.BlockSpec(b2.shape, const),
            pl.BlockSpec((tm, LANE), row),
            pl.BlockSpec((tm, LANE), row),
        ],
        out_specs=tuple(pl.BlockSpec((tm, s.shape[1]), row) for s in out_shapes),
        out_shape=out_shapes,
        compiler_params=_cparams(("arbitrary",)),
        name="inproj",
    )(x3, mods_l, norm_g, w_in_b, q_gain, k_gain, w2, b2, cos_t, sin_t)


def _attn_kernel(n_ctx_qtiles, kv_tile, n_kv_ctx, n_kv_all, q_ref, k_ref, v_ref, o_ref, m_ref, l_ref, acc_ref):
    tq = q_ref.shape[0]
    q = jnp.concatenate([q_ref[:, g * HEAD_DIM:(g + 1) * HEAD_DIM] for g in range(ATTN_GROUP)], axis=0)
    m_ref[...] = jnp.full(m_ref.shape, -jnp.inf, F32)
    l_ref[...] = jnp.zeros(l_ref.shape, F32)
    acc_ref[...] = jnp.zeros(acc_ref.shape, F32)
    reps = kv_tile // LANE

    def body(j, carry):
        start = pl.multiple_of(j * kv_tile, kv_tile)
        kt = k_ref[pl.ds(start, kv_tile), :]
        vt = v_ref[pl.ds(start, kv_tile), :]
        s = _dot_nt(q, kt)
        m_prev = m_ref[...]
        m_new = jnp.maximum(m_prev, jnp.max(s, axis=1, keepdims=True))
        alpha = jnp.exp2(m_prev - m_new)
        p = jnp.exp2(s - jnp.tile(m_new, (1, reps)))
        l_ref[...] = alpha * l_ref[...] + jnp.sum(p, axis=1, keepdims=True)
        acc_ref[...] = alpha * acc_ref[...] + _dot(p.astype(BF16), vt)
        m_ref[...] = m_new
        return carry

    n_kv = jnp.where(pl.program_id(1) < n_ctx_qtiles, n_kv_ctx, n_kv_all)
    lax.fori_loop(0, n_kv, body, 0)
    o = acc_ref[...] / l_ref[...]
    o_ref[...] = jnp.concatenate([o[g * tq:(g + 1) * tq, :] for g in range(ATTN_GROUP)], axis=1).astype(BF16)


def _attention(q, k, v, n_ctx, tq, kv_tile):
    rows = q.shape[0]
    gw = ATTN_GROUP * HEAD_DIM
    return pl.pallas_call(
        functools.partial(_attn_kernel, n_ctx // tq, kv_tile, n_ctx // kv_tile, rows // kv_tile),
        grid=(ATTN_KV_HEADS, rows // tq),
        in_specs=[
            pl.BlockSpec((tq, gw), lambda g, i: (i, g)),
            pl.BlockSpec((rows, HEAD_DIM), lambda g, i: (0, g)),
            pl.BlockSpec((rows, HEAD_DIM), lambda g, i: (0, g)),
        ],
        out_specs=pl.BlockSpec((tq, gw), lambda g, i: (i, g)),
        out_shape=jax.ShapeDtypeStruct((rows, ATTN_WIDTH), BF16),
        scratch_shapes=[pltpu.VMEM((ATTN_GROUP * tq, LANE), F32)] * 3,
        compiler_params=_cparams(("arbitrary", "arbitrary")),
        name="attention",
    )(q, k, v)


def _gla_direction(reverse, q_ref, k_ref, v_ref, la_ref, o_ref, s_ref):
    n = GLA_CHUNK
    row = lax.broadcasted_iota(jnp.int32, (n, n), 0)
    col = lax.broadcasted_iota(jnp.int32, (n, n), 1)
    tri = jnp.where((col >= row) if reverse else (col <= row), 1.0, 0.0).astype(BF16)
    pieces = _split3(la_ref[...])
    b_all = _dot(tri, pieces[0]) + _dot(tri, pieces[1]) + _dot(tri, pieces[2])
    valid_pair = (col >= row) if reverse else (col <= row)
    rows1 = lax.broadcasted_iota(jnp.int32, (n, 1), 0)
    last = 0 if reverse else n - 1
    for h in range(GLA_HEADS):
        ks = slice(h * GLA_DK, (h + 1) * GLA_DK)
        q = q_ref[:, ks]
        k = k_ref[:, ks]
        v = v_ref[:, h * GLA_DV:(h + 1) * GLA_DV]
        b = b_all[:, ks]
        b_tot = b[last:last + 1, :]
        a_rows = []
        for blk in range(n // GLA_SUB):
            lo, hi = blk * GLA_SUB, (blk + 1) * GLA_SUB
            ref_row = hi - 1 if reverse else lo
            b_ref = b[ref_row:ref_row + 1, :]
            q_blk = q[lo:hi, :] * jnp.exp(b[lo:hi, :] - b_ref)
            reach = (rows1 >= lo) if reverse else (rows1 < hi)
            k_blk = jnp.where(reach, k * jnp.exp(jnp.where(reach, b_ref - b, 0.0)), 0.0)
            a_rows.append(_dot_nt(q_blk.astype(BF16), k_blk.astype(BF16)))
        a = jnp.where(valid_pair, jnp.concatenate(a_rows, axis=0), 0.0).astype(BF16)
        st = s_ref[h]
        o = _dot(a, v) + _dot_nt((q * jnp.exp(b)).astype(BF16), st.astype(BF16))
        o_ref[:, h * GLA_DV:(h + 1) * GLA_DV] = o
        k_end = (k * jnp.exp(b_tot - b)).astype(BF16)
        s_ref[h] = jnp.exp(b_tot) * st + _dot_tn(v, k_end)


def _gla_kernel(qf, kf, vf, laf, qb, kb, vb, lab, of_ref, ob_ref, sf_ref, sb_ref):
    @pl.when(pl.program_id(0) == 0)
    def _():
        sf_ref[...] = jnp.zeros(sf_ref.shape, F32)
        sb_ref[...] = jnp.zeros(sb_ref.shape, F32)

    _gla_direction(False, qf, kf, vf, laf, of_ref, sf_ref)
    _gla_direction(True, qb, kb, vb, lab, ob_ref, sb_ref)


def _gla(gq, gk, gv, la, n_ctx):
    rows = gq.shape[0]
    n_chunks = rows // GLA_CHUNK
    n_ctx_chunks = n_ctx // GLA_CHUNK

    def bchunk(s):
        return jnp.where(s < n_ctx_chunks, n_ctx_chunks - 1 - s, n_chunks - 1 - s + n_ctx_chunks)

    fwd = lambda s: (s, 0)
    bwd = lambda s: (bchunk(s), 0)
    bwd_la = lambda s: (bchunk(s), 1)
    key_blk = (GLA_CHUNK, GLA_KEY_WIDTH)
    val_blk = (GLA_CHUNK, GLA_VALUE_WIDTH)
    state = pltpu.VMEM((GLA_HEADS, GLA_DV, GLA_DK), F32)
    return pl.pallas_call(
        _gla_kernel,
        grid=(n_chunks,),
        in_specs=[
            pl.BlockSpec(key_blk, fwd), pl.BlockSpec(key_blk, fwd), pl.BlockSpec(val_blk, fwd),
            pl.BlockSpec(key_blk, fwd),
            pl.BlockSpec(key_blk, bwd), pl.BlockSpec(key_blk, bwd), pl.BlockSpec(val_blk, bwd),
            pl.BlockSpec(key_blk, bwd_la),
        ],
        out_specs=(pl.BlockSpec(val_blk, fwd), pl.BlockSpec(val_blk, bwd)),
        out_shape=(jax.ShapeDtypeStruct((rows, GLA_VALUE_WIDTH), F32),) * 2,
        scratch_shapes=[state, state],
        compiler_params=_cparams(("arbitrary",)),
        name="gla_scan",
    )(gq, gk, gv, la, gq, gk, gv, la)


def _outproj_kernel(n_ctx_tiles, a_ref, of_ref, ob_ref, r_ref, x_ref, m_ref, gg_ref, w_ref, nf_ref,
                    wrh_ref, wrl_ref, xo_ref, h2_ref, aff_ref):
    is_ctx = pl.program_id(0) < n_ctx_tiles
    gla = of_ref[...] + ob_ref[...]
    r = r_ref[...]
    parts = [a_ref[...]]
    for h in range(GLA_HEADS):
        sl = slice(h * GLA_DV, (h + 1) * GLA_DV)
        parts.append((_rms(gla[:, sl], gg_ref[...]) * _silu(r[:, sl])).astype(BF16))
    y = _dot(jnp.concatenate(parts, axis=1), w_ref[...])
    x = _load_rows(x_ref) + _mod_rows(m_ref, is_ctx, 2) * y
    _store_rows(xo_ref, x)
    h2 = _rms(x, nf_ref[...]) * (1.0 + _mod_rows(m_ref, is_ctx, 4)) + _mod_rows(m_ref, is_ctx, 3)
    _store_rows(h2_ref, h2)
    h_hi, h_lo = _split2(h2)
    logits = _dot(h_hi, wrh_ref[...]) + _dot(h_lo, wrh_ref[...]) + _dot(h_hi, wrl_ref[...])
    live = lax.broadcasted_iota(jnp.int32, logits.shape, 1) < N_EXPERTS
    logits = jnp.where(live, logits, -jnp.inf)
    e = jnp.exp(logits - jnp.max(logits, axis=-1, keepdims=True))
    aff_ref[...] = e / jnp.sum(e, axis=-1, keepdims=True)


def _outproj(attn_o, o_f, o_b, r, x3, mods_l, gla_gain, w_out_b, norm_ffn, wr_hi, wr_lo, tm, n_ctx_tiles):
    rows = x3.shape[0]
    row = lambda i: (i, 0)
    row3 = lambda i: (i, 0, 0)
    const = lambda i: (0, 0)
    return pl.pallas_call(
        functools.partial(_outproj_kernel, n_ctx_tiles),
        grid=(rows // tm,),
        in_specs=[
            pl.BlockSpec((tm, ATTN_WIDTH), row),
            pl.BlockSpec((tm, GLA_VALUE_WIDTH), row),
            pl.BlockSpec((tm, GLA_VALUE_WIDTH), row),
            pl.BlockSpec((tm, GLA_VALUE_WIDTH), row),
            pl.BlockSpec((tm, SLABS, LANE), row3),
            pl.BlockSpec(mods_l.shape, const),
            pl.BlockSpec((1, GLA_DV), const),
            pl.BlockSpec(w_out_b.shape, const, pipeline_mode=pl.Buffered(1)),
            pl.BlockSpec((1, D_MODEL), const),
            pl.BlockSpec(wr_hi.shape, const),
            pl.BlockSpec(wr_lo.shape, const),
        ],
        out_specs=(
            pl.BlockSpec((tm, SLABS, LANE), row3),
            pl.BlockSpec((tm, SLABS, LANE), row3),
            pl.BlockSpec((tm, LANE), row),
        ),
        out_shape=(
            jax.ShapeDtypeStruct(x3.shape, F32),
            jax.ShapeDtypeStruct(x3.shape, F32),
            jax.ShapeDtypeStruct((rows, LANE), F32),
        ),
        compiler_params=_cparams(("arbitrary",)),
        name="outproj",
    )(attn_o, o_f, o_b, r, x3, mods_l, gla_gain, w_out_b, norm_ffn, wr_hi, wr_lo)


def _exclusive_rank(mask_b):
    nr = mask_b.shape[0]
    li = lax.broadcasted_iota(jnp.int32, (LANE, LANE), 0)
    lj = lax.broadcasted_iota(jnp.int32, (LANE, LANE), 1)
    within = _dot(mask_b, jnp.where(li < lj, 1.0, 0.0).astype(BF16))
    ri = lax.broadcasted_iota(jnp.int32, (nr, nr), 0)
    rj = lax.broadcasted_iota(jnp.int32, (nr, nr), 1)
    before = _dot(jnp.where(rj < ri, 1.0, 0.0).astype(BF16), mask_b)
    return within + jnp.sum(before, axis=1, keepdims=True)


def _route_kernel(cap, aff_ref, out_ref, pos_ref, acc_ref):
    a = aff_ref[0]
    bits = pltpu.bitcast(a, jnp.int32)
    nr = a.shape[0]

    def count_ge(t):
        return jnp.sum(jnp.sum(jnp.where(bits >= t, 1.0, 0.0), axis=1, keepdims=True), axis=0, keepdims=True)

    def search(i, t):
        cand = t | (jnp.int32(1) << (30 - i))
        return jnp.where(count_ge(cand) >= cap, cand, t)

    thr = lax.fori_loop(0, 31, search, jnp.zeros((1, 1), jnp.int32))
    above = bits > thr
    tie = bits == thr
    n_above = jnp.sum(jnp.sum(jnp.where(above, 1.0, 0.0), axis=1, keepdims=True), axis=0, keepdims=True)
    tie_rank = _exclusive_rank(jnp.where(tie, 1.0, 0.0).astype(BF16))
    chosen = above | (tie & (tie_rank < cap - n_above))
    slot = _exclusive_rank(jnp.where(chosen, 1.0, 0.0).astype(BF16))
    pos_ref[...] = jnp.where(chosen, slot, -1.0)
    acc_ref[...] = jnp.zeros(acc_ref.shape, F32)
    slot_ids = lax.broadcasted_iota(jnp.int32, (cap, LANE), 0).astype(F32)
    lane_row = lax.broadcasted_iota(jnp.int32, (1, LANE), 1).astype(F32)
    pad_rows = jnp.zeros((LANE - SUBLANE, LANE), BF16)

    def gather_row(r, carry):
        onehot = jnp.where(slot_ids == pos_ref[pl.ds(r, 1), :], 1.0, 0.0).astype(BF16)
        a_hi, a_mid, a_lo = _split3(aff_ref[0, pl.ds(r, 1), :])
        r_row = jnp.full((1, LANE), r, jnp.int32).astype(F32)
        vals = jnp.concatenate(
            [lane_row.astype(BF16), r_row.astype(BF16), a_hi, a_mid, a_lo, jnp.zeros((3, LANE), BF16), pad_rows],
            axis=0)
        acc_ref[...] += _dot_nt(onehot, vals)
        return carry

    lax.fori_loop(0, nr, gather_row, 0)
    acc = acc_ref[...]
    token = acc[:, 0:1] + LANE * acc[:, 1:2]
    gate = acc[:, 2:3] + acc[:, 3:4] + acc[:, 4:5]
    lane = lax.broadcasted_iota(jnp.int32, (cap, LANE), 1)
    out_ref[0] = jnp.where(lane == 0, token, jnp.where(lane == 1, gate, 0.0))


def _route(aff_t, cap):
    ne, nr, _ = aff_t.shape
    assert nr <= 256 and nr % SUBLANE == 0
    return pl.pallas_call(
        functools.partial(_route_kernel, cap),
        grid=(ne,),
        in_specs=[pl.BlockSpec((1, nr, LANE), lambda e: (e, 0, 0))],
        out_specs=pl.BlockSpec((1, cap, LANE), lambda e: (e, 0, 0)),
        out_shape=jax.ShapeDtypeStruct((ne, cap, LANE), F32),
        scratch_shapes=[pltpu.VMEM((nr, LANE), F32), pltpu.VMEM((cap, LANE), F32)],
        compiler_params=_cparams(("arbitrary",)),
        name="route",
    )(aff_t)


def _route_tokens(aff, lo, n, offset):
    cap = CAPACITY_FACTOR * n // N_EXPERTS
    a = aff[lo:lo + n, :N_EXPERTS].T.reshape(N_EXPERTS, n // LANE, LANE)
    pad = (-a.shape[1]) % SUBLANE
    if pad:
        a = jnp.concatenate([a, jnp.full((N_EXPERTS, pad, LANE), -1.0, F32)], axis=1)
    sel = _route(a, cap)
    return sel[:, :, 0].astype(jnp.int32) + offset, sel[:, :, 1]


def _moe_kernel(n_lat, idx_ref, gate_ref, m_ref, wg_ref, wu_ref, wd_ref, h2_hbm, x_hbm, xo_hbm,
                stage, xg, acc, sem):
    del x_hbm
    e = pl.program_id(0)
    f = pl.program_id(1)
    ns = stage.shape[0]

    def start_rows(to_vmem, hbm):
        def body(s, carry):
            t = idx_ref[e, s]
            if to_vmem:
                pltpu.make_async_copy(hbm.at[t], stage.at[s], sem).start()
            else:
                pltpu.make_async_copy(stage.at[s], hbm.at[t], sem).start()
            return carry
        lax.fori_loop(0, ns, body, 0)

    def wait_rows(hbm):
        pltpu.make_async_copy(hbm.at[pl.ds(0, ns)], stage, sem).wait()

    @pl.when(f == 0)
    def _():
        start_rows(True, h2_hbm)
        wait_rows(h2_hbm)
        xg[...] = _load_rows(stage).astype(BF16)

    x = xg[...]
    hid = (_silu(_dot(x, wg_ref[0].astype(BF16))) * _dot(x, wu_ref[0].astype(BF16))).astype(BF16)
    part = _dot(hid, wd_ref[0].astype(BF16))

    @pl.when(f == 0)
    def _():
        acc[...] = part

    @pl.when(f > 0)
    def _():
        acc[...] += part

    @pl.when(f == pl.num_programs(1) - 1)
    def _():
        lo, hi = 5 * D_MODEL, 6 * D_MODEL
        slot = lax.broadcasted_iota(jnp.int32, (ns, 1), 0)
        g2 = jnp.where(slot < n_lat, m_ref[0:1, lo:hi], m_ref[1:2, lo:hi])
        y = acc[...] * gate_ref[0] * g2
        start_rows(True, xo_hbm)
        wait_rows(xo_hbm)
        for j in range(SLABS):
            stage[:, j, :] = stage[:, j, :] + y[:, j * LANE:(j + 1) * LANE]
        start_rows(False, xo_hbm)
        wait_rows(xo_hbm)


def _moe(idx, gate, mods_l, w_gate, w_up, w_down, h2, x3, n_lat):
    ne, ns = idx.shape
    ft = 256
    nf = EXPERT_FF // ft
    grid_spec = pltpu.PrefetchScalarGridSpec(
        num_scalar_prefetch=1,
        grid=(ne, nf),
        in_specs=[
            pl.BlockSpec((1, ns, 1), lambda e, f, idx: (e, 0, 0)),
            pl.BlockSpec(mods_l.shape, lambda e, f, idx: (0, 0)),
            pl.BlockSpec((1, D_MODEL, ft), lambda e, f, idx: (e, 0, f)),
            pl.BlockSpec((1, D_MODEL, ft), lambda e, f, idx: (e, 0, f)),
            pl.BlockSpec((1, ft, D_MODEL), lambda e, f, idx: (e, f, 0)),
            pl.BlockSpec(memory_space=pl.ANY),
            pl.BlockSpec(memory_space=pl.ANY),
        ],
        out_specs=pl.BlockSpec(memory_space=pl.ANY),
        scratch_shapes=[
            pltpu.VMEM((ns, SLABS, LANE), F32),
            pltpu.VMEM((ns, D_MODEL), BF16),
            pltpu.VMEM((ns, D_MODEL), F32),
            pltpu.SemaphoreType.DMA,
        ],
    )
    return pl.pallas_call(
        functools.partial(_moe_kernel, n_lat),
        grid_spec=grid_spec,
        out_shape=jax.ShapeDtypeStruct(x3.shape, F32),
        input_output_aliases={7: 0},
        compiler_params=_cparams(("arbitrary", "arbitrary")),
        name="moe",
    )(idx, gate[:, :, None], mods_l, w_gate, w_up, w_down, h2, x3)


def _final_kernel(x_ref, g_ref, o_ref):
    o_ref[...] = _rms(_load_rows(x_ref), g_ref[...])


def _final_norm(x3, gain, n_ctx, tm):
    n_lat = x3.shape[0] - n_ctx
    skip = n_ctx // tm
    return pl.pallas_call(
        _final_kernel,
        grid=(n_lat // tm,),
        in_specs=[
            pl.BlockSpec((tm, SLABS, LANE), lambda i: (i + skip, 0, 0)),
            pl.BlockSpec((1, D_MODEL), lambda i: (0, 0)),
        ],
        out_specs=pl.BlockSpec((tm, D_MODEL), lambda i: (i, 0)),
        out_shape=jax.ShapeDtypeStruct((n_lat, D_MODEL), F32),
        compiler_params=_cparams(("arbitrary",)),
        name="final_norm",
    )(x3, gain)


def _rope_tables(n_lat, n_ctx):
    rows = n_lat // GRID_W
    row = jnp.repeat(jnp.arange(rows, dtype=F32), GRID_W)
    col = jnp.tile(jnp.arange(GRID_W, dtype=F32), rows)
    half = HEAD_DIM // 2
    inv_freq = ROPE_THETA ** (-jnp.arange(0, half, 2, dtype=F32) / half)
    ang = jnp.concatenate([row[:, None] * inv_freq, col[:, None] * inv_freq], axis=-1)
    cos = jnp.repeat(jnp.cos(ang), 2, axis=-1)
    sign = jnp.tile(jnp.array([-1.0, 1.0], F32), half)
    sin = jnp.repeat(jnp.sin(ang), 2, axis=-1) * sign
    cos = jnp.concatenate([jnp.ones((n_ctx, HEAD_DIM), F32), cos], axis=0)
    sin = jnp.concatenate([jnp.zeros((n_ctx, HEAD_DIM), F32), sin], axis=0)
    return cos, sin


def _row_tile(n_lat, n_ctx):
    for tm in (256, 128, 64):
        if n_lat % tm == 0 and n_ctx % tm == 0:
            return tm
    raise ValueError("token counts must be multiples of 64")


def kernel(x, c, ctx, c_ctx, w_mod, b_mod, norm_mix, w_in, q_gain, k_gain, w_gla_a2, b_gla_a, gla_gain,
           w_out, norm_ffn, w_router, w_gate, w_up, w_down, final_norm):
    batch, n_lat, _ = x.shape
    n_ctx = ctx.shape[1]
    depth = w_mod.shape[0]
    assert batch == 1 and c.shape[0] == 1
    tm = _row_tile(n_lat, n_ctx)
    n_ctx_tiles = n_ctx // tm
    tq = 128 if n_ctx % 128 == 0 else 64
    kv_tile = 256 if n_ctx % 256 == 0 else 128

    cvec = jnp.concatenate([c, c_ctx[None, :], jnp.zeros((SUBLANE - 2, D_MODEL), F32)], axis=0)
    mods = _modulation(cvec, w_mod, b_mod)
    cos_t, sin_t = _rope_tables(n_lat, n_ctx)
    x3 = jnp.concatenate([ctx[0], x[0]], axis=0).reshape(n_ctx + n_lat, SLABS, LANE)

    for l in range(depth):
        last = l == depth - 1
        w_in_b = jnp.pad(w_in[l], ((0, 0), (0, IN_WIDTH_PAD - IN_WIDTH))).astype(BF16)
        w2 = jnp.zeros((LANE, 2 * GLA_KEY_WIDTH), F32)
        o0 = IN_WIDTH - 2 * GLA_GATE_RANK - GATE_COL
        w2 = w2.at[o0:o0 + GLA_GATE_RANK, :GLA_KEY_WIDTH].set(w_gla_a2[l, 0])
        w2 = w2.at[o0 + GLA_GATE_RANK:o0 + 2 * GLA_GATE_RANK, GLA_KEY_WIDTH:].set(w_gla_a2[l, 1])
        b2 = b_gla_a[l].reshape(1, 2 * GLA_KEY_WIDTH)
        q, k, v, gq, gk, gv, r, la = _inproj(
            x3, mods[l], norm_mix[l][None, :], w_in_b, q_gain[l][None, :], k_gain[l][None, :], w2, b2,
            cos_t, sin_t, tm, n_ctx_tiles)
        attn_o = _attention(q, k, v, n_ctx, tq, kv_tile)
        o_f, o_b = _gla(gq, gk, gv, la, n_ctx)
        wr = jnp.pad(w_router[l], ((0, 0), (0, LANE - N_EXPERTS)))
        wr_hi, wr_lo = _split2(wr)
        x3, h2, aff = _outproj(attn_o, o_f, o_b, r, x3, mods[l], gla_gain[l][None, :], w_out[l].astype(BF16),
                               norm_ffn[l][None, :], wr_hi, wr_lo, tm, n_ctx_tiles)
        idx, gate = _route_tokens(aff, n_ctx, n_lat, n_ctx)
        if not last:
            idx_c, gate_c = _route_tokens(aff, 0, n_ctx, 0)
            idx = jnp.concatenate([idx, idx_c], axis=1)
            gate = jnp.concatenate([gate, gate_c], axis=1)
        x3 = _moe(idx, gate, mods[l], w_gate[l], w_up[l], w_down[l], h2, x3, idx.shape[1] if last else idx.shape[1] - idx_c.shape[1])

    return _final_norm(x3, final_norm[None, :], n_ctx, tm)[None]
```

```python
import functools
import math

import jax
import jax.numpy as jnp
from jax import lax
from jax.experimental import pallas as pl
from jax.experimental.pallas import tpu as pltpu

D_MODEL = 2048
HEAD_DIM = 128
ATTN_HEADS = 8
ATTN_KV_HEADS = 2
ATTN_GROUP = ATTN_HEADS // ATTN_KV_HEADS
ATTN_WIDTH = ATTN_HEADS * HEAD_DIM
KV_WIDTH = ATTN_KV_HEADS * HEAD_DIM
GLA_HEADS = 4
GLA_DK = 128
GLA_DV = 256
GLA_KEY_WIDTH = GLA_HEADS * GLA_DK
GLA_VALUE_WIDTH = GLA_HEADS * GLA_DV
GLA_GATE_RANK = 16
GLA_GATE_TEMP = 16.0
GLA_CHUNK = 64
GLA_SUB = 16
N_EXPERTS = 16
EXPERT_FF = 1536
CAPACITY_FACTOR = 2
GRID_W = 64
ROPE_THETA = 10000.0
EPS = 1e-6
IN_WIDTH = ATTN_WIDTH + 2 * KV_WIDTH + 2 * GLA_KEY_WIDTH + 2 * GLA_VALUE_WIDTH + 2 * GLA_GATE_RANK

LANE = 128
SUBLANE = 8
VMEM_LIMIT_BYTES = 56 * 1024 * 1024

IN_WIDTH_PAD = ((IN_WIDTH + LANE - 1) // LANE) * LANE
GATE_COL = IN_WIDTH_PAD - LANE

F32 = jnp.float32
BF16 = jnp.bfloat16


def _cparams(sem):
    return pltpu.CompilerParams(dimension_semantics=sem, vmem_limit_bytes=VMEM_LIMIT_BYTES)


def _dot(a, b):
    return jnp.dot(a, b, preferred_element_type=F32)


def _dot_nt(a, b):
    return lax.dot_general(a, b, (((1,), (1,)), ((), ())), preferred_element_type=F32)


def _dot_tn(a, b):
    return lax.dot_general(a, b, (((0,), (0,)), ((), ())), preferred_element_type=F32)


def _split2(a):
    hi = a.astype(BF16)
    lo = (a - hi.astype(F32)).astype(BF16)
    return hi, lo


def _split3(a):
    hi = a.astype(BF16)
    r1 = a - hi.astype(F32)
    mid = r1.astype(BF16)
    lo = (r1 - mid.astype(F32)).astype(BF16)
    return hi, mid, lo


def _dot_split(a, w):
    a_hi, a_lo = _split2(a)
    w_hi, w_lo = _split2(w)
    return _dot(a_hi, w_hi) + _dot(a_lo, w_hi) + _dot(a_hi, w_lo)


def _sigmoid(x):
    return 1.0 / (1.0 + jnp.exp(-x))


def _silu(x):
    return x * _sigmoid(x)


def _rms(x, gain):
    return x * lax.rsqrt(jnp.mean(x * x, axis=-1, keepdims=True) + EPS) * gain


def _mod_kernel(c_ref, w_ref, b_ref, o_ref):
    o_ref[0] = _dot_split(_silu(c_ref[...]), w_ref[0]) + b_ref[0]


def _modulation(cvec, w_mod, b_mod):
    depth, _, width = w_mod.shape
    tile = 1536
    return pl.pallas_call(
        _mod_kernel,
        grid=(depth, width // tile),
        in_specs=[
            pl.BlockSpec((SUBLANE, D_MODEL), lambda l, j: (0, 0)),
            pl.BlockSpec((1, D_MODEL, tile), lambda l, j: (l, 0, j)),
            pl.BlockSpec((1, 1, tile), lambda l, j: (l, 0, j)),
        ],
        out_specs=pl.BlockSpec((1, SUBLANE, tile), lambda l, j: (l, 0, j)),
        out_shape=jax.ShapeDtypeStruct((depth, SUBLANE, width), F32),
        compiler_params=_cparams(("arbitrary", "arbitrary")),
        name="modulation",
    )(cvec, w_mod, b_mod[:, None, :])


def _mod_rows(m_ref, is_ctx, k):
    lo, hi = k * D_MODEL, (k + 1) * D_MODEL
    return jnp.where(is_ctx, m_ref[1:2, lo:hi], m_ref[0:1, lo:hi])


def _rope(x, cos, sin_signed, even):
    partner = jnp.where(even, pltpu.roll(x, LANE - 1, 1), pltpu.roll(x, 1, 1))
    return x * cos + partner * sin_signed


def _inproj_kernel(n_ctx_tiles, x_ref, m_ref, g_ref, w_ref, qg_ref, kg_ref, w2_ref, b2_ref, cos_ref, sin_ref,
                   q_ref, k_ref, v_ref, gq_ref, gk_ref, gv_ref, r_ref, la_ref):
    is_ctx = pl.program_id(0) < n_ctx_tiles
    h = _rms(x_ref[...], g_ref[...]) * (1.0 + _mod_rows(m_ref, is_ctx, 1)) + _mod_rows(m_ref, is_ctx, 0)
    hb = h.astype(BF16)
    cos = cos_ref[...]
    sin = sin_ref[...]
    even = (lax.broadcasted_iota(jnp.int32, cos.shape, 1) % 2) == 0
    q_scale = HEAD_DIM ** -0.5 * math.log2(math.e)

    def head(p, gain, scale):
        y = _rope(_rms(p, gain), cos, sin, even)
        return (y * scale).astype(BF16) if scale != 1.0 else y.astype(BF16)

    pq = _dot(hb, w_ref[:, 0:ATTN_WIDTH])
    for hh in range(ATTN_HEADS):
        sl = slice(hh * HEAD_DIM, (hh + 1) * HEAD_DIM)
        q_ref[:, sl] = head(pq[:, sl], qg_ref[...], q_scale)
    c0 = ATTN_WIDTH
    pk = _dot(hb, w_ref[:, c0:c0 + KV_WIDTH])
    for hh in range(ATTN_KV_HEADS):
        sl = slice(hh * HEAD_DIM, (hh + 1) * HEAD_DIM)
        k_ref[:, sl] = head(pk[:, sl], kg_ref[...], 1.0)
    c0 += KV_WIDTH
    v_ref[...] = _dot(hb, w_ref[:, c0:c0 + KV_WIDTH]).astype(BF16)
    c0 += KV_WIDTH
    gq_ref[...] = _dot(hb, w_ref[:, c0:c0 + GLA_KEY_WIDTH]) * (GLA_DK ** -0.5)
    c0 += GLA_KEY_WIDTH
    gk_ref[...] = _dot(hb, w_ref[:, c0:c0 + GLA_KEY_WIDTH])
    c0 += GLA_KEY_WIDTH
    gv_ref[...] = _dot(hb, w_ref[:, c0:c0 + GLA_VALUE_WIDTH]).astype(BF16)
    c0 += GLA_VALUE_WIDTH
    r_ref[...] = _dot(hb, w_ref[:, c0:c0 + GLA_VALUE_WIDTH])
    a = _dot(hb, w_ref[:, GATE_COL:GATE_COL + LANE])
    z = _dot_split(a, w2_ref[...]) + b2_ref[...]
    la_ref[...] = (jnp.minimum(z, 0.0) - jnp.log(1.0 + jnp.exp(-jnp.abs(z)))) * (1.0 / GLA_GATE_TEMP)


def _inproj(x2, mods_l, norm_g, w_in_b, q_gain, k_gain, w2, b2, cos_t, sin_t, tm, n_ctx_tiles):
    rows = x2.shape[0]
    row = lambda i: (i, 0)
    const = lambda i: (0, 0)
    out_shapes = (
        jax.ShapeDtypeStruct((rows, ATTN_WIDTH), BF16),
        jax.ShapeDtypeStruct((rows, KV_WIDTH), BF16),
        jax.ShapeDtypeStruct((rows, KV_WIDTH), BF16),
        jax.ShapeDtypeStruct((rows, GLA_KEY_WIDTH), F32),
        jax.ShapeDtypeStruct((rows, GLA_KEY_WIDTH), F32),
        jax.ShapeDtypeStruct((rows, GLA_VALUE_WIDTH), BF16),
        jax.ShapeDtypeStruct((rows, GLA_VALUE_WIDTH), F32),
        jax.ShapeDtypeStruct((rows, 2 * GLA_KEY_WIDTH), F32),
    )
    return pl.pallas_call(
        functools.partial(_inproj_kernel, n_ctx_tiles),
        grid=(rows // tm,),
        in_specs=[
            pl.BlockSpec((tm, D_MODEL), row),
            pl.BlockSpec(mods_l.shape, const),
            pl.BlockSpec((1, D_MODEL), const),
            pl.BlockSpec(w_in_b.shape, const, pipeline_mode=pl.Buffered(1)),
            pl.BlockSpec((1, HEAD_DIM), const),
            pl.BlockSpec((1, HEAD_DIM), const),
            pl.BlockSpec(w2.shape, const),
            pl.BlockSpec(b2.shape, const),
            pl.BlockSpec((tm, LANE), row),
            pl.BlockSpec((tm, LANE), row),
        ],
        out_specs=tuple(pl.BlockSpec((tm, s.shape[1]), row) for s in out_shapes),
        out_shape=out_shapes,
        compiler_params=_cparams(("arbitrary",)),
        name="inproj",
    )(x2, mods_l, norm_g, w_in_b, q_gain, k_gain, w2, b2, cos_t, sin_t)


def _with_ones(v):
    return jnp.concatenate([v, jnp.ones_like(v)], axis=1)


def _attn_kernel(n_ctx_qtiles, n_ctx, kv_tile, n_pairs, q_ref, k_ref, v_ref, o_ref, s0_ref, s1_ref, m_ref, acc_ref):
    tq = q_ref.shape[0]
    q = jnp.concatenate([q_ref[:, g * HEAD_DIM:(g + 1) * HEAD_DIM] for g in range(ATTN_GROUP)], axis=0)
    is_ctx = pl.program_id(1) < n_ctx_qtiles

    def finish(acc):
        o = acc[:, :HEAD_DIM] / acc[:, HEAD_DIM:]
        o_ref[...] = jnp.concatenate([o[g * tq:(g + 1) * tq, :] for g in range(ATTN_GROUP)], axis=1).astype(BF16)

    @pl.when(is_ctx)
    def _():
        s = _dot_nt(q, k_ref[0:n_ctx, :])
        p = jnp.exp2(s - jnp.max(s, axis=1, keepdims=True)).astype(BF16)
        finish(_dot(p, _with_ones(v_ref[0:n_ctx, :])))

    @pl.when(jnp.logical_not(is_ctx))
    def _():
        reps = kv_tile // LANE
        m_ref[...] = jnp.full(m_ref.shape, -jnp.inf, F32)
        acc_ref[...] = jnp.zeros(acc_ref.shape, F32)

        def scores(tile):
            start = pl.multiple_of(tile * kv_tile, LANE)
            return _dot_nt(q, k_ref[pl.ds(start, kv_tile), :])

        def update(s_ref, tile):
            start = pl.multiple_of(tile * kv_tile, LANE)
            s = s_ref[...]
            m_prev = m_ref[...]
            m_new = jnp.maximum(m_prev, jnp.max(s, axis=1, keepdims=True))
            alpha = jnp.exp2(m_prev - m_new)
            p = jnp.exp2(s - jnp.tile(m_new, (1, reps))).astype(BF16)
            acc_ref[...] = jnp.tile(alpha, (1, 2)) * acc_ref[...] + _dot(p, _with_ones(v_ref[pl.ds(start, kv_tile), :]))
            m_ref[...] = m_new

        s0_ref[...] = scores(0)

        def pair(jj, carry):
            s1_ref[...] = scores(2 * jj + 1)
            update(s0_ref, 2 * jj)
            s0_ref[...] = scores(jnp.minimum(2 * jj + 2, 2 * n_pairs - 1))
            update(s1_ref, 2 * jj + 1)
            return carry

        lax.fori_loop(0, n_pairs, pair, 0)
        finish(acc_ref[...])


def _attention(q, k, v, n_ctx, tq, kv_tile):
    rows = q.shape[0]
    gw = ATTN_GROUP * HEAD_DIM
    n_tiles = rows // kv_tile
    score = pltpu.VMEM((ATTN_GROUP * tq, kv_tile), F32)
    return pl.pallas_call(
        functools.partial(_attn_kernel, n_ctx // tq, n_ctx, kv_tile, n_tiles // 2),
        grid=(ATTN_KV_HEADS, rows // tq),
        in_specs=[
            pl.BlockSpec((tq, gw), lambda g, i: (i, g)),
            pl.BlockSpec((rows, HEAD_DIM), lambda g, i: (0, g)),
            pl.BlockSpec((rows, HEAD_DIM), lambda g, i: (0, g)),
        ],
        out_specs=pl.BlockSpec((tq, gw), lambda g, i: (i, g)),
        out_shape=jax.ShapeDtypeStruct((rows, ATTN_WIDTH), BF16),
        scratch_shapes=[score, score, pltpu.VMEM((ATTN_GROUP * tq, LANE), F32),
                        pltpu.VMEM((ATTN_GROUP * tq, 2 * HEAD_DIM), F32)],
        compiler_params=_cparams(("arbitrary", "arbitrary")),
        name="attention",
    )(q, k, v)


def _kv_tile(rows):
    for n in range(1536 // LANE, 0, -1):
        if rows % (n * LANE) == 0 and (rows // (n * LANE)) % 2 == 0:
            return n * LANE
    raise ValueError("unsupported token count for the attention key tiling")


def _gla_direction(reverse, q_ref, k_ref, v_ref, la_ref, o_ref, s_ref):
    n = GLA_CHUNK
    row = lax.broadcasted_iota(jnp.int32, (n, n), 0)
    col = lax.broadcasted_iota(jnp.int32, (n, n), 1)
    tri = jnp.where((col >= row) if reverse else (col <= row), 1.0, 0.0).astype(BF16)
    pieces = _split3(la_ref[...])
    b_all = _dot(tri, pieces[0]) + _dot(tri, pieces[1]) + _dot(tri, pieces[2])
    valid_pair = (col >= row) if reverse else (col <= row)
    rows1 = lax.broadcasted_iota(jnp.int32, (n, 1), 0)
    last = 0 if reverse else n - 1
    for h in range(GLA_HEADS):
        ks = slice(h * GLA_DK, (h + 1) * GLA_DK)
        q = q_ref[:, ks]
        k = k_ref[:, ks]
        v = v_ref[:, h * GLA_DV:(h + 1) * GLA_DV]
        b = b_all[:, ks]
        b_tot = b[last:last + 1, :]
        a_rows = []
        for blk in range(n // GLA_SUB):
            lo, hi = blk * GLA_SUB, (blk + 1) * GLA_SUB
            ref_row = hi - 1 if reverse else lo
            b_ref = b[ref_row:ref_row + 1, :]
            q_blk = q[lo:hi, :] * jnp.exp(b[lo:hi, :] - b_ref)
            reach = (rows1 >= lo) if reverse else (rows1 < hi)
            k_blk = jnp.where(reach, k * jnp.exp(jnp.where(reach, b_ref - b, 0.0)), 0.0)
            a_rows.append(_dot_nt(q_blk.astype(BF16), k_blk.astype(BF16)))
        a = jnp.where(valid_pair, jnp.concatenate(a_rows, axis=0), 0.0).astype(BF16)
        st = s_ref[h]
        o = _dot(a, v) + _dot_nt((q * jnp.exp(b)).astype(BF16), st.astype(BF16))
        o_ref[:, h * GLA_DV:(h + 1) * GLA_DV] = o
        k_end = (k * jnp.exp(b_tot - b)).astype(BF16)
        s_ref[h] = jnp.exp(b_tot) * st + _dot_tn(v, k_end)


def _gla_kernel(qf, kf, vf, laf, qb, kb, vb, lab, of_ref, ob_ref, sf_ref, sb_ref):
    @pl.when(pl.program_id(0) == 0)
    def _():
        sf_ref[...] = jnp.zeros(sf_ref.shape, F32)
        sb_ref[...] = jnp.zeros(sb_ref.shape, F32)

    _gla_direction(False, qf, kf, vf, laf, of_ref, sf_ref)
    _gla_direction(True, qb, kb, vb, lab, ob_ref, sb_ref)


def _gla(gq, gk, gv, la, n_ctx):
    rows = gq.shape[0]
    n_chunks = rows // GLA_CHUNK
    n_ctx_chunks = n_ctx // GLA_CHUNK

    def bchunk(s):
        return jnp.where(s < n_ctx_chunks, n_ctx_chunks - 1 - s, n_chunks - 1 - s + n_ctx_chunks)

    fwd = lambda s: (s, 0)
    bwd = lambda s: (bchunk(s), 0)
    bwd_la = lambda s: (bchunk(s), 1)
    key_blk = (GLA_CHUNK, GLA_KEY_WIDTH)
    val_blk = (GLA_CHUNK, GLA_VALUE_WIDTH)
    state = pltpu.VMEM((GLA_HEADS, GLA_DV, GLA_DK), F32)
    return pl.pallas_call(
        _gla_kernel,
        grid=(n_chunks,),
        in_specs=[
            pl.BlockSpec(key_blk, fwd), pl.BlockSpec(key_blk, fwd), pl.BlockSpec(val_blk, fwd),
            pl.BlockSpec(key_blk, fwd),
            pl.BlockSpec(key_blk, bwd), pl.BlockSpec(key_blk, bwd), pl.BlockSpec(val_blk, bwd),
            pl.BlockSpec(key_blk, bwd_la),
        ],
        out_specs=(pl.BlockSpec(val_blk, fwd), pl.BlockSpec(val_blk, bwd)),
        out_shape=(jax.ShapeDtypeStruct((rows, GLA_VALUE_WIDTH), F32),) * 2,
        scratch_shapes=[state, state],
        compiler_params=_cparams(("arbitrary",)),
        name="gla_scan",
    )(gq, gk, gv, la, gq, gk, gv, la)


def _outproj_kernel(n_ctx_tiles, a_ref, of_ref, ob_ref, r_ref, x_ref, m_ref, gg_ref, w_ref, nf_ref,
                    wrh_ref, wrl_ref, xo_ref, h2_ref, aff_ref):
    is_ctx = pl.program_id(0) < n_ctx_tiles
    gla = of_ref[...] + ob_ref[...]
    r = r_ref[...]
    parts = [a_ref[...]]
    for h in range(GLA_HEADS):
        sl = slice(h * GLA_DV, (h + 1) * GLA_DV)
        parts.append((_rms(gla[:, sl], gg_ref[...]) * _silu(r[:, sl])).astype(BF16))
    y = _dot(jnp.concatenate(parts, axis=1), w_ref[...])
    x = x_ref[...] + _mod_rows(m_ref, is_ctx, 2) * y
    xo_ref[...] = x
    h2 = _rms(x, nf_ref[...]) * (1.0 + _mod_rows(m_ref, is_ctx, 4)) + _mod_rows(m_ref, is_ctx, 3)
    h2_ref[...] = h2
    h_hi, h_lo = _split2(h2)
    logits = _dot(h_hi, wrh_ref[...]) + _dot(h_lo, wrh_ref[...]) + _dot(h_hi, wrl_ref[...])
    live = lax.broadcasted_iota(jnp.int32, logits.shape, 1) < N_EXPERTS
    logits = jnp.where(live, logits, -jnp.inf)
    e = jnp.exp(logits - jnp.max(logits, axis=-1, keepdims=True))
    aff_ref[...] = e / jnp.sum(e, axis=-1, keepdims=True)


def _outproj(attn_o, o_f, o_b, r, x2, mods_l, gla_gain, w_out_b, norm_ffn, wr_hi, wr_lo, tm, n_ctx_tiles):
    rows = x2.shape[0]
    row = lambda i: (i, 0)
    const = lambda i: (0, 0)
    return pl.pallas_call(
        functools.partial(_outproj_kernel, n_ctx_tiles),
        grid=(rows // tm,),
        in_specs=[
            pl.BlockSpec((tm, ATTN_WIDTH), row),
            pl.BlockSpec((tm, GLA_VALUE_WIDTH), row),
            pl.BlockSpec((tm, GLA_VALUE_WIDTH), row),
            pl.BlockSpec((tm, GLA_VALUE_WIDTH), row),
            pl.BlockSpec((tm, D_MODEL), row),
            pl.BlockSpec(mods_l.shape, const),
            pl.BlockSpec((1, GLA_DV), const),
            pl.BlockSpec(w_out_b.shape, const, pipeline_mode=pl.Buffered(1)),
            pl.BlockSpec((1, D_MODEL), const),
            pl.BlockSpec(wr_hi.shape, const),
            pl.BlockSpec(wr_lo.shape, const),
        ],
        out_specs=(
            pl.BlockSpec((tm, D_MODEL), row),
            pl.BlockSpec((tm, D_MODEL), row),
            pl.BlockSpec((tm, LANE), row),
        ),
        out_shape=(
            jax.ShapeDtypeStruct(x2.shape, F32),
            jax.ShapeDtypeStruct(x2.shape, F32),
            jax.ShapeDtypeStruct((rows, LANE), F32),
        ),
        compiler_params=_cparams(("arbitrary",)),
        name="outproj",
    )(attn_o, o_f, o_b, r, x2, mods_l, gla_gain, w_out_b, norm_ffn, wr_hi, wr_lo)


def _exclusive_rank(mask_b):
    nr = mask_b.shape[0]
    li = lax.broadcasted_iota(jnp.int32, (LANE, LANE), 0)
    lj = lax.broadcasted_iota(jnp.int32, (LANE, LANE), 1)
    within = _dot(mask_b, jnp.where(li < lj, 1.0, 0.0).astype(BF16))
    ri = lax.broadcasted_iota(jnp.int32, (nr, nr), 0)
    rj = lax.broadcasted_iota(jnp.int32, (nr, nr), 1)
    before = jnp.sum(_dot(jnp.where(rj < ri, 1.0, 0.0).astype(BF16), mask_b), axis=1, keepdims=True)
    return within + before, before


SEARCH_STEPS = 150
COMPACT_WINDOW = LANE + 16


def _select_kernel(cap, aff_ref, pos_ref, off_ref):
    a = aff_ref[...]
    ne = a.shape[0]

    def count(mask):
        return jnp.sum(jnp.sum(jnp.where(mask, 1.0, 0.0), axis=2, keepdims=True), axis=1, keepdims=True)

    def search(_, carry):
        t, step = carry
        cand = t + step
        return jnp.where(count(a >= cand) >= cap, cand, t), step * 0.5

    thr, _ = lax.fori_loop(0, SEARCH_STEPS, search, (jnp.zeros((ne, 1, 1), F32), jnp.ones((1, 1, 1), F32)))
    room = cap - count(a > thr)
    for e in range(ne):
        above = a[e] > thr[e]
        tie = a[e] == thr[e]
        tie_rank, _ = _exclusive_rank(jnp.where(tie, 1.0, 0.0).astype(BF16))
        chosen = above | (tie & (tie_rank < room[e]))
        slot, before = _exclusive_rank(jnp.where(chosen, 1.0, 0.0).astype(BF16))
        pos_ref[e] = jnp.where(chosen, slot, -1.0)
        off_ref[e] = jnp.broadcast_to(before, slot.shape)


def _compact_kernel(cap, off_ref, pos_ref, aff_ref, out_ref, acc_ref):
    e = pl.program_id(0)
    nr = pos_ref.shape[1]
    acc_ref[...] = jnp.zeros(acc_ref.shape, F32)
    rel_ids = lax.broadcasted_iota(jnp.int32, (COMPACT_WINDOW, LANE), 0).astype(F32)
    lane_row = lax.broadcasted_iota(jnp.int32, (1, LANE), 1).astype(F32)
    pad_rows = jnp.zeros((LANE - SUBLANE, LANE), BF16)

    def gather_row(r, carry):
        start = pl.multiple_of(jnp.minimum((off_ref[e, r] // SUBLANE) * SUBLANE, cap), SUBLANE)
        slot_ids = rel_ids + start.astype(F32)
        onehot = jnp.where(slot_ids == pos_ref[0, pl.ds(r, 1), :], 1.0, 0.0).astype(BF16)
        a_hi, a_mid, a_lo = _split3(aff_ref[0, pl.ds(r, 1), :])
        r_row = jnp.full((1, LANE), r, jnp.int32).astype(F32)
        vals = jnp.concatenate(
            [lane_row.astype(BF16), r_row.astype(BF16), a_hi, a_mid, a_lo, jnp.zeros((3, LANE), BF16), pad_rows],
            axis=0)
        acc_ref[pl.ds(start, COMPACT_WINDOW), :] += _dot_nt(onehot, vals)
        return carry

    lax.fori_loop(0, nr, gather_row, 0)
    acc = acc_ref[0:cap, :]
    token = acc[:, 0:1] + LANE * acc[:, 1:2]
    gate = acc[:, 2:3] + acc[:, 3:4] + acc[:, 4:5]
    lane = lax.broadcasted_iota(jnp.int32, (cap, LANE), 1)
    out_ref[0] = jnp.where(lane == 0, token, jnp.where(lane == 1, gate, 0.0))


def _route(aff_t, cap):
    ne, nr, _ = aff_t.shape
    assert nr <= 256 and nr % SUBLANE == 0 and cap % SUBLANE == 0
    whole = pl.BlockSpec((ne, nr, LANE), lambda i: (0, 0, 0))
    pos, off = pl.pallas_call(
        functools.partial(_select_kernel, cap),
        grid=(1,),
        in_specs=[whole],
        out_specs=(whole, whole),
        out_shape=(jax.ShapeDtypeStruct(aff_t.shape, F32),) * 2,
        compiler_params=_cparams(("arbitrary",)),
        name="route_select",
    )(aff_t)
    per_expert = pl.BlockSpec((1, nr, LANE), lambda e, off: (e, 0, 0))
    return pl.pallas_call(
        functools.partial(_compact_kernel, cap),
        grid_spec=pltpu.PrefetchScalarGridSpec(
            num_scalar_prefetch=1,
            grid=(ne,),
            in_specs=[per_expert, per_expert],
            out_specs=pl.BlockSpec((1, cap, LANE), lambda e, off: (e, 0, 0)),
            scratch_shapes=[pltpu.VMEM((cap + COMPACT_WINDOW, LANE), F32)],
        ),
        out_shape=jax.ShapeDtypeStruct((ne, cap, LANE), F32),
        compiler_params=_cparams(("arbitrary",)),
        name="route_compact",
    )(off[:, :, 0].astype(jnp.int32), pos, aff_t)


def _route_tokens(aff, lo, n, offset):
    cap = CAPACITY_FACTOR * n // N_EXPERTS
    a = aff[lo:lo + n, :N_EXPERTS].T.reshape(N_EXPERTS, n // LANE, LANE)
    pad = (-a.shape[1]) % SUBLANE
    if pad:
        a = jnp.concatenate([a, jnp.full((N_EXPERTS, pad, LANE), -1.0, F32)], axis=1)
    sel = _route(a, cap)
    return sel[:, :, 0].astype(jnp.int32) + offset, sel[:, :, 1]


def _moe_kernel(n_lat, nf, idx_ref, gate_ref, m_ref, wg_ref, wu_ref, wd_ref, h2_hbm, x_hbm, xo_hbm,
                stage_h, stage_x, xg, acc, sem_h, sem_in, sem_out):
    del x_hbm
    e = pl.program_id(0)
    f = pl.program_id(1)
    ne = pl.num_programs(0)
    ns = stage_h.shape[0]
    per_step = ns // nf

    def start_rows(expert, lo, n, to_vmem, hbm, stage, sem, unrolled):
        def one(s):
            t = idx_ref[expert, s]
            if to_vmem:
                pltpu.make_async_copy(hbm.at[pl.ds(t, 1), :], stage.at[pl.ds(s, 1), :], sem).start()
            else:
                pltpu.make_async_copy(stage.at[pl.ds(s, 1), :], hbm.at[pl.ds(t, 1), :], sem).start()
        if unrolled:
            for i in range(n):
                one(lo + i)
        else:
            lax.fori_loop(lo, lo + n, lambda s, c: (one(s), c)[1], 0, unroll=8)

    def wait_rows(hbm, stage, sem):
        pltpu.make_async_copy(hbm.at[pl.ds(0, ns), :], stage, sem).wait()

    @pl.when(f == 0)
    def _():
        @pl.when(e == 0)
        def _():
            start_rows(0, 0, ns, True, h2_hbm, stage_h, sem_h, False)

        wait_rows(h2_hbm, stage_h, sem_h)
        xg[...] = stage_h[...].astype(BF16)
        acc[...] = jnp.zeros(acc.shape, F32)

    @pl.when(f == nf - 2)
    def _():
        @pl.when(e > 0)
        def _():
            wait_rows(xo_hbm, stage_x, sem_out)

        start_rows(e, 0, ns, True, xo_hbm, stage_x, sem_in, False)

    start_rows(jnp.minimum(e + 1, ne - 1), f * per_step, per_step, True, h2_hbm, stage_h, sem_h, True)
    x = xg[...]
    hid = (_silu(_dot(x, wg_ref[0, 0].astype(BF16))) * _dot(x, wu_ref[0, 0].astype(BF16))).astype(BF16)
    down_cols = 4 * LANE
    for n in range(D_MODEL // down_cols):
        cols = slice(n * down_cols, (n + 1) * down_cols)
        part = _dot(hid, wd_ref[0, 0, :, cols].astype(BF16))
        acc[:, cols] += part

    @pl.when(f == nf - 1)
    def _():
        is_lat = lax.broadcasted_iota(jnp.int32, (ns, 1), 0) < n_lat
        gate = gate_ref[0]
        wait_rows(xo_hbm, stage_x, sem_in)
        for n in range(D_MODEL // down_cols):
            cols = slice(n * down_cols, (n + 1) * down_cols)
            mcols = slice(5 * D_MODEL + n * down_cols, 5 * D_MODEL + (n + 1) * down_cols)
            scale = gate * jnp.where(is_lat, m_ref[0:1, mcols], m_ref[1:2, mcols])
            stage_x[:, cols] = stage_x[:, cols] + acc[:, cols] * scale
        start_rows(e, 0, ns, False, xo_hbm, stage_x, sem_out, False)

        @pl.when(e == ne - 1)
        def _():
            wait_rows(xo_hbm, stage_x, sem_out)
            wait_rows(h2_hbm, stage_h, sem_h)


def _moe(layer, idx, gate, mods_l, w_gate, w_up, w_down, h2, x2, n_lat):
    ne, ns = idx.shape
    ft = 256 if ns % (EXPERT_FF // 256) == 0 else 384
    nf = EXPERT_FF // ft
    assert nf >= 2 and ns % nf == 0
    grid_spec = pltpu.PrefetchScalarGridSpec(
        num_scalar_prefetch=1,
        grid=(ne, nf),
        in_specs=[
            pl.BlockSpec((1, ns, 1), lambda e, f, idx: (e, 0, 0)),
            pl.BlockSpec(mods_l.shape, lambda e, f, idx: (0, 0)),
            pl.BlockSpec((1, 1, D_MODEL, ft), lambda e, f, idx: (layer, e, 0, f)),
            pl.BlockSpec((1, 1, D_MODEL, ft), lambda e, f, idx: (layer, e, 0, f)),
            pl.BlockSpec((1, 1, ft, D_MODEL), lambda e, f, idx: (layer, e, f, 0)),
            pl.BlockSpec(memory_space=pl.ANY),
            pl.BlockSpec(memory_space=pl.ANY),
        ],
        out_specs=pl.BlockSpec(memory_space=pl.ANY),
        scratch_shapes=[
            pltpu.VMEM((ns, D_MODEL), F32),
            pltpu.VMEM((ns, D_MODEL), F32),
            pltpu.VMEM((ns, D_MODEL), BF16),
            pltpu.VMEM((ns, D_MODEL), F32),
            pltpu.SemaphoreType.DMA,
            pltpu.SemaphoreType.DMA,
            pltpu.SemaphoreType.DMA,
        ],
    )
    return pl.pallas_call(
        functools.partial(_moe_kernel, n_lat, nf),
        grid_spec=grid_spec,
        out_shape=jax.ShapeDtypeStruct(x2.shape, F32),
        input_output_aliases={7: 0},
        compiler_params=_cparams(("arbitrary", "arbitrary")),
        name="moe",
    )(idx, gate[:, :, None], mods_l, w_gate, w_up, w_down, h2, x2)


def _final_kernel(x_ref, g_ref, o_ref):
    o_ref[...] = _rms(x_ref[...], g_ref[...])


def _final_norm(x2, gain, n_ctx, tm):
    n_lat = x2.shape[0] - n_ctx
    skip = n_ctx // tm
    return pl.pallas_call(
        _final_kernel,
        grid=(n_lat // tm,),
        in_specs=[
            pl.BlockSpec((tm, D_MODEL), lambda i: (i + skip, 0)),
            pl.BlockSpec((1, D_MODEL), lambda i: (0, 0)),
        ],
        out_specs=pl.BlockSpec((tm, D_MODEL), lambda i: (i, 0)),
        out_shape=jax.ShapeDtypeStruct((n_lat, D_MODEL), F32),
        compiler_params=_cparams(("arbitrary",)),
        name="final_norm",
    )(x2, gain)


def _rope_tables(n_lat, n_ctx):
    rows = n_lat // GRID_W
    row = jnp.repeat(jnp.arange(rows, dtype=F32), GRID_W)
    col = jnp.tile(jnp.arange(GRID_W, dtype=F32), rows)
    half = HEAD_DIM // 2
    inv_freq = ROPE_THETA ** (-jnp.arange(0, half, 2, dtype=F32) / half)
    ang = jnp.concatenate([row[:, None] * inv_freq, col[:, None] * inv_freq], axis=-1)
    cos = jnp.repeat(jnp.cos(ang), 2, axis=-1)
    sign = jnp.tile(jnp.array([-1.0, 1.0], F32), half)
    sin = jnp.repeat(jnp.sin(ang), 2, axis=-1) * sign
    cos = jnp.concatenate([jnp.ones((n_ctx, HEAD_DIM), F32), cos], axis=0)
    sin = jnp.concatenate([jnp.zeros((n_ctx, HEAD_DIM), F32), sin], axis=0)
    return cos, sin


def _row_tile(n_lat, n_ctx):
    for tm in (256, 128, 64):
        if n_lat % tm == 0 and n_ctx % tm == 0:
            return tm
    raise ValueError("token counts must be multiples of 64")


def kernel(x, c, ctx, c_ctx, w_mod, b_mod, norm_mix, w_in, q_gain, k_gain, w_gla_a2, b_gla_a, gla_gain,
           w_out, norm_ffn, w_router, w_gate, w_up, w_down, final_norm):
    batch, n_lat, _ = x.shape
    n_ctx = ctx.shape[1]
    depth = w_mod.shape[0]
    assert batch == 1 and c.shape[0] == 1
    tm = _row_tile(n_lat, n_ctx)
    n_ctx_tiles = n_ctx // tm
    tq = 128 if n_ctx % 128 == 0 else 64
    kv_tile = _kv_tile(n_ctx + n_lat)

    cvec = jnp.concatenate([c, c_ctx[None, :], jnp.zeros((SUBLANE - 2, D_MODEL), F32)], axis=0)
    mods = _modulation(cvec, w_mod, b_mod)
    cos_t, sin_t = _rope_tables(n_lat, n_ctx)
    x2 = jnp.concatenate([ctx[0], x[0]], axis=0)

    for l in range(depth):
        last = l == depth - 1
        w_in_b = jnp.pad(w_in[l], ((0, 0), (0, IN_WIDTH_PAD - IN_WIDTH))).astype(BF16)
        w2 = jnp.zeros((LANE, 2 * GLA_KEY_WIDTH), F32)
        o0 = IN_WIDTH - 2 * GLA_GATE_RANK - GATE_COL
        w2 = w2.at[o0:o0 + GLA_GATE_RANK, :GLA_KEY_WIDTH].set(w_gla_a2[l, 0])
        w2 = w2.at[o0 + GLA_GATE_RANK:o0 + 2 * GLA_GATE_RANK, GLA_KEY_WIDTH:].set(w_gla_a2[l, 1])
        b2 = b_gla_a[l].reshape(1, 2 * GLA_KEY_WIDTH)
        q, k, v, gq, gk, gv, r, la = _inproj(
            x2, mods[l], norm_mix[l][None, :], w_in_b, q_gain[l][None, :], k_gain[l][None, :], w2, b2,
            cos_t, sin_t, tm, n_ctx_tiles)
        attn_o = _attention(q, k, v, n_ctx, tq, kv_tile)
        o_f, o_b = _gla(gq, gk, gv, la, n_ctx)
        wr = jnp.pad(w_router[l], ((0, 0), (0, LANE - N_EXPERTS)))
        wr_hi, wr_lo = _split2(wr)
        x2, h2, aff = _outproj(attn_o, o_f, o_b, r, x2, mods[l], gla_gain[l][None, :], w_out[l].astype(BF16),
                               norm_ffn[l][None, :], wr_hi, wr_lo, tm, n_ctx_tiles)
        idx, gate = _route_tokens(aff, n_ctx, n_lat, n_ctx)
        n_lat_slots = idx.shape[1]
        if not last:
            idx_c, gate_c = _route_tokens(aff, 0, n_ctx, 0)
            idx = jnp.concatenate([idx, idx_c], axis=1)
            gate = jnp.concatenate([gate, gate_c], axis=1)
        x2 = _moe(l, idx, gate, mods[l], w_gate, w_up, w_down, h2, x2, n_lat_slots)

    return _final_norm(x2, final_norm[None, :], n_ctx, tm)[None]
```

```python
import functools
import math

import jax
import jax.numpy as jnp
from jax import lax
from jax.experimental import pallas as pl
from jax.experimental.pallas import tpu as pltpu

D_MODEL = 2048
HEAD_DIM = 128
ATTN_HEADS = 8
ATTN_KV_HEADS = 2
ATTN_GROUP = ATTN_HEADS // ATTN_KV_HEADS
ATTN_WIDTH = ATTN_HEADS * HEAD_DIM
KV_WIDTH = ATTN_KV_HEADS * HEAD_DIM
GLA_HEADS = 4
GLA_DK = 128
GLA_DV = 256
GLA_KEY_WIDTH = GLA_HEADS * GLA_DK
GLA_VALUE_WIDTH = GLA_HEADS * GLA_DV
GLA_GATE_RANK = 16
GLA_GATE_TEMP = 16.0
GLA_CHUNK = 64
GLA_SUB = 16
N_EXPERTS = 16
EXPERT_FF = 1536
CAPACITY_FACTOR = 2
GRID_W = 64
ROPE_THETA = 10000.0
EPS = 1e-6
IN_WIDTH = ATTN_WIDTH + 2 * KV_WIDTH + 2 * GLA_KEY_WIDTH + 2 * GLA_VALUE_WIDTH + 2 * GLA_GATE_RANK

LANE = 128
SUBLANE = 8
VMEM_LIMIT_BYTES = 56 * 1024 * 1024

IN_WIDTH_PAD = ((IN_WIDTH + LANE - 1) // LANE) * LANE
GATE_COL = IN_WIDTH_PAD - LANE

F32 = jnp.float32
BF16 = jnp.bfloat16


def _cparams(sem):
    return pltpu.CompilerParams(dimension_semantics=sem, vmem_limit_bytes=VMEM_LIMIT_BYTES)


def _dot(a, b):
    return jnp.dot(a, b, preferred_element_type=F32)


def _dot_nt(a, b):
    return lax.dot_general(a, b, (((1,), (1,)), ((), ())), preferred_element_type=F32)


def _dot_tn(a, b):
    return lax.dot_general(a, b, (((0,), (0,)), ((), ())), preferred_element_type=F32)


def _split2(a):
    hi = a.astype(BF16)
    lo = (a - hi.astype(F32)).astype(BF16)
    return hi, lo


def _split3(a):
    hi = a.astype(BF16)
    r1 = a - hi.astype(F32)
    mid = r1.astype(BF16)
    lo = (r1 - mid.astype(F32)).astype(BF16)
    return hi, mid, lo


def _dot_split(a, w):
    a_hi, a_lo = _split2(a)
    w_hi, w_lo = _split2(w)
    return _dot(a_hi, w_hi) + _dot(a_lo, w_hi) + _dot(a_hi, w_lo)


def _sigmoid(x):
    return 1.0 / (1.0 + jnp.exp(-x))


def _silu(x):
    return x * _sigmoid(x)


def _rms(x, gain):
    return x * lax.rsqrt(jnp.mean(x * x, axis=-1, keepdims=True) + EPS) * gain


def _mod_kernel(c_ref, w_ref, b_ref, o_ref):
    o_ref[0] = _dot_split(_silu(c_ref[...]), w_ref[0]) + b_ref[0]


def _modulation(cvec, w_mod, b_mod):
    depth, _, width = w_mod.shape
    tile = 1536
    return pl.pallas_call(
        _mod_kernel,
        grid=(depth, width // tile),
        in_specs=[
            pl.BlockSpec((SUBLANE, D_MODEL), lambda l, j: (0, 0)),
            pl.BlockSpec((1, D_MODEL, tile), lambda l, j: (l, 0, j)),
            pl.BlockSpec((1, 1, tile), lambda l, j: (l, 0, j)),
        ],
        out_specs=pl.BlockSpec((1, SUBLANE, tile), lambda l, j: (l, 0, j)),
        out_shape=jax.ShapeDtypeStruct((depth, SUBLANE, width), F32),
        compiler_params=_cparams(("arbitrary", "arbitrary")),
        name="modulation",
    )(cvec, w_mod, b_mod[:, None, :])


def _mod_rows(m_ref, is_ctx, k):
    lo, hi = k * D_MODEL, (k + 1) * D_MODEL
    return jnp.where(is_ctx, m_ref[1:2, lo:hi], m_ref[0:1, lo:hi])


def _rope(x, cos, sin_signed, even):
    partner = jnp.where(even, pltpu.roll(x, LANE - 1, 1), pltpu.roll(x, 1, 1))
    return x * cos + partner * sin_signed


def _inproj_kernel(n_ctx_tiles, x_ref, m_ref, g_ref, w_ref, qg_ref, kg_ref, w2_ref, b2_ref, cos_ref, sin_ref,
                   q_ref, k_ref, v_ref, gq_ref, gk_ref, gv_ref, r_ref, la_ref):
    is_ctx = pl.program_id(0) < n_ctx_tiles
    h = _rms(x_ref[...], g_ref[...]) * (1.0 + _mod_rows(m_ref, is_ctx, 1)) + _mod_rows(m_ref, is_ctx, 0)
    hb = h.astype(BF16)
    cos = cos_ref[...]
    sin = sin_ref[...]
    even = (lax.broadcasted_iota(jnp.int32, cos.shape, 1) % 2) == 0
    q_scale = HEAD_DIM ** -0.5 * math.log2(math.e)

    def head(p, gain, scale):
        y = _rope(_rms(p, gain), cos, sin, even)
        return (y * scale).astype(BF16) if scale != 1.0 else y.astype(BF16)

    pq = _dot(hb, w_ref[:, 0:ATTN_WIDTH])
    for hh in range(ATTN_HEADS):
        sl = slice(hh * HEAD_DIM, (hh + 1) * HEAD_DIM)
        q_ref[:, sl] = head(pq[:, sl], qg_ref[...], q_scale)
    c0 = ATTN_WIDTH
    pk = _dot(hb, w_ref[:, c0:c0 + KV_WIDTH])
    for hh in range(ATTN_KV_HEADS):
        sl = slice(hh * HEAD_DIM, (hh + 1) * HEAD_DIM)
        k_ref[:, sl] = head(pk[:, sl], kg_ref[...], 1.0)
    c0 += KV_WIDTH
    v_ref[...] = _dot(hb, w_ref[:, c0:c0 + KV_WIDTH]).astype(BF16)
    c0 += KV_WIDTH
    gq_ref[...] = _dot(hb, w_ref[:, c0:c0 + GLA_KEY_WIDTH]) * (GLA_DK ** -0.5)
    c0 += GLA_KEY_WIDTH
    gk_ref[...] = _dot(hb, w_ref[:, c0:c0 + GLA_KEY_WIDTH])
    c0 += GLA_KEY_WIDTH
    gv_ref[...] = _dot(hb, w_ref[:, c0:c0 + GLA_VALUE_WIDTH]).astype(BF16)
    c0 += GLA_VALUE_WIDTH
    r_ref[...] = _dot(hb, w_ref[:, c0:c0 + GLA_VALUE_WIDTH])
    a = _dot(hb, w_ref[:, GATE_COL:GATE_COL + LANE])
    z = _dot_split(a, w2_ref[...]) + b2_ref[...]
    la_ref[...] = (jnp.minimum(z, 0.0) - jnp.log(1.0 + jnp.exp(-jnp.abs(z)))) * (1.0 / GLA_GATE_TEMP)


def _inproj(x2, mods_l, norm_g, w_in_b, q_gain, k_gain, w2, b2, cos_t, sin_t, tm, n_ctx_tiles):
    rows = x2.shape[0]
    row = lambda i: (i, 0)
    const = lambda i: (0, 0)
    out_shapes = (
        jax.ShapeDtypeStruct((rows, ATTN_WIDTH), BF16),
        jax.ShapeDtypeStruct((rows, KV_WIDTH), BF16),
        jax.ShapeDtypeStruct((rows, KV_WIDTH), BF16),
        jax.ShapeDtypeStruct((rows, GLA_KEY_WIDTH), F32),
        jax.ShapeDtypeStruct((rows, GLA_KEY_WIDTH), F32),
        jax.ShapeDtypeStruct((rows, GLA_VALUE_WIDTH), BF16),
        jax.ShapeDtypeStruct((rows, GLA_VALUE_WIDTH), F32),
        jax.ShapeDtypeStruct((rows, 2 * GLA_KEY_WIDTH), F32),
    )
    return pl.pallas_call(
        functools.partial(_inproj_kernel, n_ctx_tiles),
        grid=(rows // tm,),
        in_specs=[
            pl.BlockSpec((tm, D_MODEL), row),
            pl.BlockSpec(mods_l.shape, const),
            pl.BlockSpec((1, D_MODEL), const),
            pl.BlockSpec(w_in_b.shape, const, pipeline_mode=pl.Buffered(1)),
            pl.BlockSpec((1, HEAD_DIM), const),
            pl.BlockSpec((1, HEAD_DIM), const),
            pl.BlockSpec(w2.shape, const),
            pl.BlockSpec(b2.shape, const),
            pl.BlockSpec((tm, LANE), row),
            pl.BlockSpec((tm, LANE), row),
        ],
        out_specs=tuple(pl.BlockSpec((tm, s.shape[1]), row) for s in out_shapes),
        out_shape=out_shapes,
        compiler_params=_cparams(("arbitrary",)),
        name="inproj",
    )(x2, mods_l, norm_g, w_in_b, q_gain, k_gain, w2, b2, cos_t, sin_t)


def _with_ones(v):
    return jnp.concatenate([v, jnp.ones_like(v)], axis=1)


def _attn_kernel(n_ctx_qtiles, n_ctx, kv_tile, n_pairs, q_ref, k_ref, v_ref, o_ref, s0_ref, s1_ref, m_ref, acc_ref):
    tq = q_ref.shape[0]
    q = jnp.concatenate([q_ref[:, g * HEAD_DIM:(g + 1) * HEAD_DIM] for g in range(ATTN_GROUP)], axis=0)
    is_ctx = pl.program_id(1) < n_ctx_qtiles

    def finish(acc):
        o = acc[:, :HEAD_DIM] / acc[:, HEAD_DIM:]
        o_ref[...] = jnp.concatenate([o[g * tq:(g + 1) * tq, :] for g in range(ATTN_GROUP)], axis=1).astype(BF16)

    @pl.when(is_ctx)
    def _():
        s = _dot_nt(q, k_ref[0:n_ctx, :])
        p = jnp.exp2(s - jnp.max(s, axis=1, keepdims=True)).astype(BF16)
        finish(_dot(p, _with_ones(v_ref[0:n_ctx, :])))

    @pl.when(jnp.logical_not(is_ctx))
    def _():
        reps = kv_tile // LANE
        m_ref[...] = jnp.full(m_ref.shape, -jnp.inf, F32)
        acc_ref[...] = jnp.zeros(acc_ref.shape, F32)

        def scores(tile):
            start = pl.multiple_of(tile * kv_tile, LANE)
            return _dot_nt(q, k_ref[pl.ds(start, kv_tile), :])

        def update(s_ref, tile):
            start = pl.multiple_of(tile * kv_tile, LANE)
            s = s_ref[...]
            m_prev = m_ref[...]
            m_new = jnp.maximum(m_prev, jnp.max(s, axis=1, keepdims=True))
            alpha = jnp.exp2(m_prev - m_new)
            p = jnp.exp2(s - jnp.tile(m_new, (1, reps))).astype(BF16)
            acc_ref[...] = jnp.tile(alpha, (1, 2)) * acc_ref[...] + _dot(p, _with_ones(v_ref[pl.ds(start, kv_tile), :]))
            m_ref[...] = m_new

        s0_ref[...] = scores(0)

        def pair(jj, carry):
            s1_ref[...] = scores(2 * jj + 1)
            update(s0_ref, 2 * jj)
            s0_ref[...] = scores(jnp.minimum(2 * jj + 2, 2 * n_pairs - 1))
            update(s1_ref, 2 * jj + 1)
            return carry

        lax.fori_loop(0, n_pairs, pair, 0)
        finish(acc_ref[...])


def _attention(q, k, v, n_ctx, tq, kv_tile):
    rows = q.shape[0]
    gw = ATTN_GROUP * HEAD_DIM
    n_tiles = rows // kv_tile
    score = pltpu.VMEM((ATTN_GROUP * tq, kv_tile), F32)
    return pl.pallas_call(
        functools.partial(_attn_kernel, n_ctx // tq, n_ctx, kv_tile, n_tiles // 2),
        grid=(ATTN_KV_HEADS, rows // tq),
        in_specs=[
            pl.BlockSpec((tq, gw), lambda g, i: (i, g)),
            pl.BlockSpec((rows, HEAD_DIM), lambda g, i: (0, g)),
            pl.BlockSpec((rows, HEAD_DIM), lambda g, i: (0, g)),
        ],
        out_specs=pl.BlockSpec((tq, gw), lambda g, i: (i, g)),
        out_shape=jax.ShapeDtypeStruct((rows, ATTN_WIDTH), BF16),
        scratch_shapes=[score, score, pltpu.VMEM((ATTN_GROUP * tq, LANE), F32),
                        pltpu.VMEM((ATTN_GROUP * tq, 2 * HEAD_DIM), F32)],
        compiler_params=_cparams(("arbitrary", "arbitrary")),
        name="attention",
    )(q, k, v)


def _kv_tile(rows):
    for n in range(1536 // LANE, 0, -1):
        if rows % (n * LANE) == 0 and (rows // (n * LANE)) % 2 == 0:
            return n * LANE
    raise ValueError("unsupported token count for the attention key tiling")


def _gla_direction(reverse, q_ref, k_ref, v_ref, la_ref, o_ref, s_ref):
    n = GLA_CHUNK
    row = lax.broadcasted_iota(jnp.int32, (n, n), 0)
    col = lax.broadcasted_iota(jnp.int32, (n, n), 1)
    tri = jnp.where((col >= row) if reverse else (col <= row), 1.0, 0.0).astype(BF16)
    valid_pair = (col >= row) if reverse else (col <= row)
    rows1 = lax.broadcasted_iota(jnp.int32, (n, 1), 0)
    last = 0 if reverse else n - 1
    n_sub = q_ref.shape[0] // n
    b_all = {}
    for c in range(n_sub):
        pieces = _split3(la_ref[c * n:(c + 1) * n, :])
        b_all[c] = _dot(tri, pieces[0]) + _dot(tri, pieces[1]) + _dot(tri, pieces[2])
    for c, h in [(c, h) for c in (range(n_sub - 1, -1, -1) if reverse else range(n_sub)) for h in range(GLA_HEADS)]:
        rs = slice(c * n, (c + 1) * n)
        ks = slice(h * GLA_DK, (h + 1) * GLA_DK)
        q = q_ref[rs, ks]
        k = k_ref[rs, ks]
        v = v_ref[rs, h * GLA_DV:(h + 1) * GLA_DV]
        b = b_all[c][:, ks]
        b_tot = b[last:last + 1, :]
        a_rows = []
        for blk in range(n // GLA_SUB):
            lo, hi = blk * GLA_SUB, (blk + 1) * GLA_SUB
            ref_row = hi - 1 if reverse else lo
            b_ref = b[ref_row:ref_row + 1, :]
            q_blk = q[lo:hi, :] * jnp.exp(b[lo:hi, :] - b_ref)
            reach = (rows1 >= lo) if reverse else (rows1 < hi)
            k_blk = jnp.where(reach, k * jnp.exp(jnp.where(reach, b_ref - b, 0.0)), 0.0)
            a_rows.append(_dot_nt(q_blk.astype(BF16), k_blk.astype(BF16)))
        a = jnp.where(valid_pair, jnp.concatenate(a_rows, axis=0), 0.0).astype(BF16)
        st = s_ref[h]
        o = _dot(a, v) + _dot_nt((q * jnp.exp(b)).astype(BF16), st.astype(BF16))
        o_ref[rs, h * GLA_DV:(h + 1) * GLA_DV] = o
        k_end = (k * jnp.exp(b_tot - b)).astype(BF16)
        s_ref[h] = jnp.exp(b_tot) * st + _dot_tn(v, k_end)


def _gla_kernel(qf, kf, vf, laf, qb, kb, vb, lab, of_ref, ob_ref, sf_ref, sb_ref):
    @pl.when(pl.program_id(0) == 0)
    def _():
        sf_ref[...] = jnp.zeros(sf_ref.shape, F32)
        sb_ref[...] = jnp.zeros(sb_ref.shape, F32)

    _gla_direction(False, qf, kf, vf, laf, of_ref, sf_ref)
    _gla_direction(True, qb, kb, vb, lab, ob_ref, sb_ref)


def _gla(gq, gk, gv, la, n_ctx):
    rows = gq.shape[0]
    step = 2 * GLA_CHUNK if n_ctx % (2 * GLA_CHUNK) == 0 and rows % (2 * GLA_CHUNK) == 0 else GLA_CHUNK
    n_chunks = rows // step
    n_ctx_chunks = n_ctx // step

    def bchunk(s):
        return jnp.where(s < n_ctx_chunks, n_ctx_chunks - 1 - s, n_chunks - 1 - s + n_ctx_chunks)

    fwd = lambda s: (s, 0)
    bwd = lambda s: (bchunk(s), 0)
    bwd_la = lambda s: (bchunk(s), 1)
    key_blk = (step, GLA_KEY_WIDTH)
    val_blk = (step, GLA_VALUE_WIDTH)
    state = pltpu.VMEM((GLA_HEADS, GLA_DV, GLA_DK), F32)
    return pl.pallas_call(
        _gla_kernel,
        grid=(n_chunks,),
        in_specs=[
            pl.BlockSpec(key_blk, fwd), pl.BlockSpec(key_blk, fwd), pl.BlockSpec(val_blk, fwd),
            pl.BlockSpec(key_blk, fwd),
            pl.BlockSpec(key_blk, bwd), pl.BlockSpec(key_blk, bwd), pl.BlockSpec(val_blk, bwd),
            pl.BlockSpec(key_blk, bwd_la),
        ],
        out_specs=(pl.BlockSpec(val_blk, fwd), pl.BlockSpec(val_blk, bwd)),
        out_shape=(jax.ShapeDtypeStruct((rows, GLA_VALUE_WIDTH), F32),) * 2,
        scratch_shapes=[state, state],
        compiler_params=_cparams(("arbitrary",)),
        name="gla_scan",
    )(gq, gk, gv, la, gq, gk, gv, la)


def _outproj_kernel(n_ctx_tiles, a_ref, of_ref, ob_ref, r_ref, x_ref, m_ref, gg_ref, w_ref, nf_ref,
                    wrh_ref, wrl_ref, xo_ref, h2_ref, aff_ref):
    is_ctx = pl.program_id(0) < n_ctx_tiles
    gla = of_ref[...] + ob_ref[...]
    r = r_ref[...]
    parts = [a_ref[...]]
    for h in range(GLA_HEADS):
        sl = slice(h * GLA_DV, (h + 1) * GLA_DV)
        parts.append((_rms(gla[:, sl], gg_ref[...]) * _silu(r[:, sl])).astype(BF16))
    y = _dot(jnp.concatenate(parts, axis=1), w_ref[...])
    x = x_ref[...] + _mod_rows(m_ref, is_ctx, 2) * y
    xo_ref[...] = x
    h2 = _rms(x, nf_ref[...]) * (1.0 + _mod_rows(m_ref, is_ctx, 4)) + _mod_rows(m_ref, is_ctx, 3)
    h2_ref[...] = h2
    h_hi, h_lo = _split2(h2)
    logits = _dot(h_hi, wrh_ref[...]) + _dot(h_lo, wrh_ref[...]) + _dot(h_hi, wrl_ref[...])
    live = lax.broadcasted_iota(jnp.int32, logits.shape, 1) < N_EXPERTS
    logits = jnp.where(live, logits, -jnp.inf)
    e = jnp.exp(logits - jnp.max(logits, axis=-1, keepdims=True))
    aff_ref[...] = e / jnp.sum(e, axis=-1, keepdims=True)


def _outproj(attn_o, o_f, o_b, r, x2, mods_l, gla_gain, w_out_b, norm_ffn, wr_hi, wr_lo, tm, n_ctx_tiles):
    rows = x2.shape[0]
    row = lambda i: (i, 0)
    const = lambda i: (0, 0)
    return pl.pallas_call(
        functools.partial(_outproj_kernel, n_ctx_tiles),
        grid=(rows // tm,),
        in_specs=[
            pl.BlockSpec((tm, ATTN_WIDTH), row),
            pl.BlockSpec((tm, GLA_VALUE_WIDTH), row),
            pl.BlockSpec((tm, GLA_VALUE_WIDTH), row),
            pl.BlockSpec((tm, GLA_VALUE_WIDTH), row),
            pl.BlockSpec((tm, D_MODEL), row),
            pl.BlockSpec(mods_l.shape, const),
            pl.BlockSpec((1, GLA_DV), const),
            pl.BlockSpec(w_out_b.shape, const, pipeline_mode=pl.Buffered(1)),
            pl.BlockSpec((1, D_MODEL), const),
            pl.BlockSpec(wr_hi.shape, const),
            pl.BlockSpec(wr_lo.shape, const),
        ],
        out_specs=(
            pl.BlockSpec((tm, D_MODEL), row),
            pl.BlockSpec((tm, D_MODEL), row),
            pl.BlockSpec((tm, LANE), row),
        ),
        out_shape=(
            jax.ShapeDtypeStruct(x2.shape, F32),
            jax.ShapeDtypeStruct(x2.shape, F32),
            jax.ShapeDtypeStruct((rows, LANE), F32),
        ),
        compiler_params=_cparams(("arbitrary",)),
        name="outproj",
    )(attn_o, o_f, o_b, r, x2, mods_l, gla_gain, w_out_b, norm_ffn, wr_hi, wr_lo)


def _exclusive_rank(mask_b):
    nr = mask_b.shape[0]
    li = lax.broadcasted_iota(jnp.int32, (LANE, LANE), 0)
    lj = lax.broadcasted_iota(jnp.int32, (LANE, LANE), 1)
    within = _dot(mask_b, jnp.where(li < lj, 1.0, 0.0).astype(BF16))
    ri = lax.broadcasted_iota(jnp.int32, (nr, nr), 0)
    rj = lax.broadcasted_iota(jnp.int32, (nr, nr), 1)
    before = jnp.sum(_dot(jnp.where(rj < ri, 1.0, 0.0).astype(BF16), mask_b), axis=1, keepdims=True)
    return within + before, before


SEARCH_STEPS = 150
COMPACT_WINDOW = LANE + 16


def _select_kernel(cap, aff_ref, pos_ref, off_ref):
    a = aff_ref[...]
    ne = a.shape[0]

    def count(mask):
        return jnp.sum(jnp.sum(jnp.where(mask, 1.0, 0.0), axis=2, keepdims=True), axis=1, keepdims=True)

    def search(_, carry):
        t, step = carry
        cand = t + step
        return jnp.where(count(a >= cand) >= cap, cand, t), step * 0.5

    thr, _ = lax.fori_loop(0, SEARCH_STEPS, search, (jnp.zeros((ne, 1, 1), F32), jnp.ones((1, 1, 1), F32)))
    room = cap - count(a > thr)
    for e in range(ne):
        above = a[e] > thr[e]
        tie = a[e] == thr[e]
        tie_rank, _ = _exclusive_rank(jnp.where(tie, 1.0, 0.0).astype(BF16))
        chosen = above | (tie & (tie_rank < room[e]))
        slot, before = _exclusive_rank(jnp.where(chosen, 1.0, 0.0).astype(BF16))
        pos_ref[e] = jnp.where(chosen, slot, -1.0)
        off_ref[e] = jnp.broadcast_to(before, slot.shape)


def _compact_kernel(cap, off_ref, pos_ref, aff_ref, out_ref, acc_ref):
    e = pl.program_id(0)
    nr = pos_ref.shape[1]
    acc_ref[...] = jnp.zeros(acc_ref.shape, F32)
    rel_ids = lax.broadcasted_iota(jnp.int32, (COMPACT_WINDOW, LANE), 0).astype(F32)
    lane_row = lax.broadcasted_iota(jnp.int32, (1, LANE), 1).astype(F32)
    pad_rows = jnp.zeros((LANE - SUBLANE, LANE), BF16)

    def gather_row(r, carry):
        start = pl.multiple_of(jnp.minimum((off_ref[e, r] // SUBLANE) * SUBLANE, cap), SUBLANE)
        slot_ids = rel_ids + start.astype(F32)
        onehot = jnp.where(slot_ids == pos_ref[0, pl.ds(r, 1), :], 1.0, 0.0).astype(BF16)
        a_hi, a_mid, a_lo = _split3(aff_ref[0, pl.ds(r, 1), :])
        r_row = jnp.full((1, LANE), r, jnp.int32).astype(F32)
        vals = jnp.concatenate(
            [lane_row.astype(BF16), r_row.astype(BF16), a_hi, a_mid, a_lo, jnp.zeros((3, LANE), BF16), pad_rows],
            axis=0)
        acc_ref[pl.ds(start, COMPACT_WINDOW), :] += _dot_nt(onehot, vals)
        return carry

    lax.fori_loop(0, nr, gather_row, 0, unroll=4)
    acc = acc_ref[0:cap, :]
    token = acc[:, 0:1] + LANE * acc[:, 1:2]
    gate = acc[:, 2:3] + acc[:, 3:4] + acc[:, 4:5]
    lane = lax.broadcasted_iota(jnp.int32, (cap, LANE), 1)
    out_ref[0] = jnp.where(lane == 0, token, jnp.where(lane == 1, gate, 0.0))


def _route(aff_t, cap):
    ne, nr, _ = aff_t.shape
    assert nr <= 256 and nr % SUBLANE == 0 and cap % SUBLANE == 0
    whole = pl.BlockSpec((ne, nr, LANE), lambda i: (0, 0, 0))
    pos, off = pl.pallas_call(
        functools.partial(_select_kernel, cap),
        grid=(1,),
        in_specs=[whole],
        out_specs=(whole, whole),
        out_shape=(jax.ShapeDtypeStruct(aff_t.shape, F32),) * 2,
        compiler_params=_cparams(("arbitrary",)),
        name="route_select",
    )(aff_t)
    per_expert = pl.BlockSpec((1, nr, LANE), lambda e, off: (e, 0, 0))
    return pl.pallas_call(
        functools.partial(_compact_kernel, cap),
        grid_spec=pltpu.PrefetchScalarGridSpec(
            num_scalar_prefetch=1,
            grid=(ne,),
            in_specs=[per_expert, per_expert],
            out_specs=pl.BlockSpec((1, cap, LANE), lambda e, off: (e, 0, 0)),
            scratch_shapes=[pltpu.VMEM((cap + COMPACT_WINDOW, LANE), F32)],
        ),
        out_shape=jax.ShapeDtypeStruct((ne, cap, LANE), F32),
        compiler_params=_cparams(("arbitrary",)),
        name="route_compact",
    )(off[:, :, 0].astype(jnp.int32), pos, aff_t)


def _route_tokens(aff, lo, n, offset):
    cap = CAPACITY_FACTOR * n // N_EXPERTS
    a = aff[lo:lo + n, :N_EXPERTS].T.reshape(N_EXPERTS, n // LANE, LANE)
    pad = (-a.shape[1]) % SUBLANE
    if pad:
        a = jnp.concatenate([a, jnp.full((N_EXPERTS, pad, LANE), -1.0, F32)], axis=1)
    sel = _route(a, cap)
    return sel[:, :, 0].astype(jnp.int32) + offset, sel[:, :, 1]


def _moe_kernel(n_lat, nf, idx_ref, gate_ref, m_ref, wg_ref, wu_ref, wd_ref, h2_hbm, x_hbm, xo_hbm,
                stage_h, stage_x, xg, acc, sem_h, sem_in, sem_out):
    del x_hbm
    e = pl.program_id(0)
    f = pl.program_id(1)
    ne = pl.num_programs(0)
    groups = stage_h.shape[0]
    ns = groups * SUBLANE
    per_step = ns // nf
    down_cols = 4 * LANE

    def row_copy(t, g, i, to_vmem, hbm, stage, sem):
        if to_vmem:
            return pltpu.make_async_copy(hbm.at[pl.ds(t, 1), :], stage.at[g, pl.ds(i, 1), :], sem)
        return pltpu.make_async_copy(stage.at[g, pl.ds(i, 1), :], hbm.at[pl.ds(t, 1), :], sem)

    def start_all_rows(expert, to_vmem, hbm, stage, sem):
        def group(g, carry):
            first = expert * ns + g * SUBLANE
            for i in range(SUBLANE):
                row_copy(idx_ref[first + i], g, i, to_vmem, hbm, stage, sem).start()
            return carry
        lax.fori_loop(0, groups, group, 0)

    def start_step_rows(expert, to_vmem, hbm, stage, sem):
        first = expert * ns + f * per_step
        for j in range(per_step):
            if per_step % SUBLANE == 0:
                g, i = f * (per_step // SUBLANE) + j // SUBLANE, j % SUBLANE
            else:
                s = f * per_step + j
                g, i = s // SUBLANE, s % SUBLANE
            row_copy(idx_ref[first + j], g, i, to_vmem, hbm, stage, sem).start()

    def wait_rows(stage, sem):
        pltpu.make_async_copy(stage, stage, sem).wait()

    @pl.when(f == 0)
    def _():
        @pl.when(e == 0)
        def _():
            start_all_rows(0, True, h2_hbm, stage_h, sem_h)

        wait_rows(stage_h, sem_h)
        xg[...] = stage_h[...].reshape(ns, D_MODEL).astype(BF16)
        acc[...] = jnp.zeros(acc.shape, F32)

    @pl.when(f == nf - 2)
    def _():
        @pl.when(e > 0)
        def _():
            wait_rows(stage_x, sem_out)

        start_all_rows(e, True, xo_hbm, stage_x, sem_in)

    start_step_rows(jnp.minimum(e + 1, ne - 1), True, h2_hbm, stage_h, sem_h)
    x = xg[...]
    hid = (_silu(_dot(x, wg_ref[0, 0].astype(BF16))) * _dot(x, wu_ref[0, 0].astype(BF16))).astype(BF16)
    for n in range(D_MODEL // down_cols):
        cols = slice(n * down_cols, (n + 1) * down_cols)
        part = _dot(hid, wd_ref[0, 0, :, cols].astype(BF16))
        acc[:, cols] += part

    @pl.when(f == nf - 1)
    def _():
        is_lat = lax.broadcasted_iota(jnp.int32, (ns, 1), 0) < n_lat
        gate = gate_ref[0]
        wait_rows(stage_x, sem_in)
        for n in range(D_MODEL // down_cols):
            cols = slice(n * down_cols, (n + 1) * down_cols)
            mcols = slice(5 * D_MODEL + n * down_cols, 5 * D_MODEL + (n + 1) * down_cols)
            scale = gate * jnp.where(is_lat, m_ref[0:1, mcols], m_ref[1:2, mcols])
            y = (acc[:, cols] * scale).reshape(groups, SUBLANE, down_cols)
            stage_x[:, :, cols] = stage_x[:, :, cols] + y
        start_all_rows(e, False, xo_hbm, stage_x, sem_out)

        @pl.when(e == ne - 1)
        def _():
            wait_rows(stage_x, sem_out)
            wait_rows(stage_h, sem_h)


def _moe(layer, idx, gate, mods_l, w_gate, w_up, w_down, h2, x2, n_lat):
    ne, ns = idx.shape
    ft = 256 if ns % (EXPERT_FF // 256) == 0 else 384
    nf = EXPERT_FF // ft
    assert nf >= 2 and ns % nf == 0 and ns % SUBLANE == 0
    stage = pltpu.VMEM((ns // SUBLANE, SUBLANE, D_MODEL), F32)
    grid_spec = pltpu.PrefetchScalarGridSpec(
        num_scalar_prefetch=1,
        grid=(ne, nf),
        in_specs=[
            pl.BlockSpec((1, ns, 1), lambda e, f, idx: (e, 0, 0)),
            pl.BlockSpec(mods_l.shape, lambda e, f, idx: (0, 0)),
            pl.BlockSpec((1, 1, D_MODEL, ft), lambda e, f, idx: (layer, e, 0, f)),
            pl.BlockSpec((1, 1, D_MODEL, ft), lambda e, f, idx: (layer, e, 0, f)),
            pl.BlockSpec((1, 1, ft, D_MODEL), lambda e, f, idx: (layer, e, f, 0)),
            pl.BlockSpec(memory_space=pl.ANY),
            pl.BlockSpec(memory_space=pl.ANY),
        ],
        out_specs=pl.BlockSpec(memory_space=pl.ANY),
        scratch_shapes=[
            stage,
            stage,
            pltpu.VMEM((ns, D_MODEL), BF16),
            pltpu.VMEM((ns, D_MODEL), F32),
            pltpu.SemaphoreType.DMA,
            pltpu.SemaphoreType.DMA,
            pltpu.SemaphoreType.DMA,
        ],
    )
    return pl.pallas_call(
        functools.partial(_moe_kernel, n_lat, nf),
        grid_spec=grid_spec,
        out_shape=jax.ShapeDtypeStruct(x2.shape, F32),
        input_output_aliases={7: 0},
        compiler_params=_cparams(("arbitrary", "arbitrary")),
        name="moe",
    )(idx.reshape(-1), gate[:, :, None], mods_l, w_gate, w_up, w_down, h2, x2)


def _final_kernel(x_ref, g_ref, o_ref):
    o_ref[...] = _rms(x_ref[...], g_ref[...])


def _final_norm(x2, gain, n_ctx, tm):
    n_lat = x2.shape[0] - n_ctx
    skip = n_ctx // tm
    return pl.pallas_call(
        _final_kernel,
        grid=(n_lat // tm,),
        in_specs=[
            pl.BlockSpec((tm, D_MODEL), lambda i: (i + skip, 0)),
            pl.BlockSpec((1, D_MODEL), lambda i: (0, 0)),
        ],
        out_specs=pl.BlockSpec((tm, D_MODEL), lambda i: (i, 0)),
        out_shape=jax.ShapeDtypeStruct((n_lat, D_MODEL), F32),
        compiler_params=_cparams(("arbitrary",)),
        name="final_norm",
    )(x2, gain)


def _rope_tables(n_lat, n_ctx):
    rows = n_lat // GRID_W
    row = jnp.repeat(jnp.arange(rows, dtype=F32), GRID_W)
    col = jnp.tile(jnp.arange(GRID_W, dtype=F32), rows)
    half = HEAD_DIM // 2
    inv_freq = ROPE_THETA ** (-jnp.arange(0, half, 2, dtype=F32) / half)
    ang = jnp.concatenate([row[:, None] * inv_freq, col[:, None] * inv_freq], axis=-1)
    cos = jnp.repeat(jnp.cos(ang), 2, axis=-1)
    sign = jnp.tile(jnp.array([-1.0, 1.0], F32), half)
    sin = jnp.repeat(jnp.sin(ang), 2, axis=-1) * sign
    cos = jnp.concatenate([jnp.ones((n_ctx, HEAD_DIM), F32), cos], axis=0)
    sin = jnp.concatenate([jnp.zeros((n_ctx, HEAD_DIM), F32), sin], axis=0)
    return cos, sin


def _row_tile(n_lat, n_ctx):
    for tm in (256, 128, 64):
        if n_lat % tm == 0 and n_ctx % tm == 0:
            return tm
    raise ValueError("token counts must be multiples of 64")


def kernel(x, c, ctx, c_ctx, w_mod, b_mod, norm_mix, w_in, q_gain, k_gain, w_gla_a2, b_gla_a, gla_gain,
           w_out, norm_ffn, w_router, w_gate, w_up, w_down, final_norm):
    batch, n_lat, _ = x.shape
    n_ctx = ctx.shape[1]
    depth = w_mod.shape[0]
    assert batch == 1 and c.shape[0] == 1
    tm = _row_tile(n_lat, n_ctx)
    n_ctx_tiles = n_ctx // tm
    tq = next(t for t in (256, 128, 64) if n_ctx % t == 0)
    kv_tile = _kv_tile(n_ctx + n_lat)

    cvec = jnp.concatenate([c, c_ctx[None, :], jnp.zeros((SUBLANE - 2, D_MODEL), F32)], axis=0)
    mods = _modulation(cvec, w_mod, b_mod)
    cos_t, sin_t = _rope_tables(n_lat, n_ctx)
    x2 = jnp.concatenate([ctx[0], x[0]], axis=0)

    for l in range(depth):
        last = l == depth - 1
        w_in_b = jnp.pad(w_in[l], ((0, 0), (0, IN_WIDTH_PAD - IN_WIDTH))).astype(BF16)
        w2 = jnp.zeros((LANE, 2 * GLA_KEY_WIDTH), F32)
        o0 = IN_WIDTH - 2 * GLA_GATE_RANK - GATE_COL
        w2 = w2.at[o0:o0 + GLA_GATE_RANK, :GLA_KEY_WIDTH].set(w_gla_a2[l, 0])
        w2 = w2.at[o0 + GLA_GATE_RANK:o0 + 2 * GLA_GATE_RANK, GLA_KEY_WIDTH:].set(w_gla_a2[l, 1])
        b2 = b_gla_a[l].reshape(1, 2 * GLA_KEY_WIDTH)
        q, k, v, gq, gk, gv, r, la = _inproj(
            x2, mods[l], norm_mix[l][None, :], w_in_b, q_gain[l][None, :], k_gain[l][None, :], w2, b2,
            cos_t, sin_t, tm, n_ctx_tiles)
        attn_o = _attention(q, k, v, n_ctx, tq, kv_tile)
        o_f, o_b = _gla(gq, gk, gv, la, n_ctx)
        wr = jnp.pad(w_router[l], ((0, 0), (0, LANE - N_EXPERTS)))
        wr_hi, wr_lo = _split2(wr)
        x2, h2, aff = _outproj(attn_o, o_f, o_b, r, x2, mods[l], gla_gain[l][None, :], w_out[l].astype(BF16),
                               norm_ffn[l][None, :], wr_hi, wr_lo, tm, n_ctx_tiles)
        idx, gate = _route_tokens(aff, n_ctx, n_lat, n_ctx)
        n_lat_slots = idx.shape[1]
        if not last:
            idx_c, gate_c = _route_tokens(aff, 0, n_ctx, 0)
            idx = jnp.concatenate([idx, idx_c], axis=1)
            gate = jnp.concatenate([gate, gate_c], axis=1)
        x2 = _moe(l, idx, gate, mods[l], w_gate, w_up, w_down, h2, x2, n_lat_slots)

    return _final_norm(x2, final_norm[None, :], n_ctx, tm)[None]
```

```python
import functools
import math

import jax
import jax.numpy as jnp
from jax import lax
from jax.experimental import pallas as pl
from jax.experimental.pallas import tpu as pltpu

D_MODEL = 2048
HEAD_DIM = 128
ATTN_HEADS = 8
ATTN_KV_HEADS = 2
ATTN_GROUP = ATTN_HEADS // ATTN_KV_HEADS
ATTN_WIDTH = ATTN_HEADS * HEAD_DIM
KV_WIDTH = ATTN_KV_HEADS * HEAD_DIM
GLA_HEADS = 4
GLA_DK = 128
GLA_DV = 256
GLA_KEY_WIDTH = GLA_HEADS * GLA_DK
GLA_VALUE_WIDTH = GLA_HEADS * GLA_DV
GLA_GATE_RANK = 16
GLA_GATE_TEMP = 16.0
GLA_CHUNK = 64
GLA_SUB = 16
N_EXPERTS = 16
EXPERT_FF = 1536
CAPACITY_FACTOR = 2
GRID_W = 64
ROPE_THETA = 10000.0
EPS = 1e-6
IN_WIDTH = ATTN_WIDTH + 2 * KV_WIDTH + 2 * GLA_KEY_WIDTH + 2 * GLA_VALUE_WIDTH + 2 * GLA_GATE_RANK

LANE = 128
SUBLANE = 8
VMEM_LIMIT_BYTES = 56 * 1024 * 1024

IN_WIDTH_PAD = ((IN_WIDTH + LANE - 1) // LANE) * LANE
GATE_COL = IN_WIDTH_PAD - LANE

F32 = jnp.float32
BF16 = jnp.bfloat16


def _cparams(sem):
    return pltpu.CompilerParams(dimension_semantics=sem, vmem_limit_bytes=VMEM_LIMIT_BYTES)


def _dot(a, b):
    return jnp.dot(a, b, preferred_element_type=F32)


def _dot_nt(a, b):
    return lax.dot_general(a, b, (((1,), (1,)), ((), ())), preferred_element_type=F32)


def _dot_tn(a, b):
    return lax.dot_general(a, b, (((0,), (0,)), ((), ())), preferred_element_type=F32)


def _split2(a):
    hi = a.astype(BF16)
    lo = (a - hi.astype(F32)).astype(BF16)
    return hi, lo


def _split3(a):
    hi = a.astype(BF16)
    r1 = a - hi.astype(F32)
    mid = r1.astype(BF16)
    lo = (r1 - mid.astype(F32)).astype(BF16)
    return hi, mid, lo


def _dot_split(a, w):
    a_hi, a_lo = _split2(a)
    w_hi, w_lo = _split2(w)
    return _dot(a_hi, w_hi) + _dot(a_lo, w_hi) + _dot(a_hi, w_lo)


def _sigmoid(x):
    return 1.0 / (1.0 + jnp.exp(-x))


def _silu(x):
    return x * _sigmoid(x)


def _rms(x, gain):
    return x * lax.rsqrt(jnp.mean(x * x, axis=-1, keepdims=True) + EPS) * gain


def _mod_kernel(c_ref, w_ref, b_ref, o_ref):
    o_ref[0] = _dot_split(_silu(c_ref[...]), w_ref[0]) + b_ref[0]


def _modulation(cvec, w_mod, b_mod):
    depth, _, width = w_mod.shape
    tile = 1536
    return pl.pallas_call(
        _mod_kernel,
        grid=(depth, width // tile),
        in_specs=[
            pl.BlockSpec((SUBLANE, D_MODEL), lambda l, j: (0, 0)),
            pl.BlockSpec((1, D_MODEL, tile), lambda l, j: (l, 0, j)),
            pl.BlockSpec((1, 1, tile), lambda l, j: (l, 0, j)),
        ],
        out_specs=pl.BlockSpec((1, SUBLANE, tile), lambda l, j: (l, 0, j)),
        out_shape=jax.ShapeDtypeStruct((depth, SUBLANE, width), F32),
        compiler_params=_cparams(("arbitrary", "arbitrary")),
        name="modulation",
    )(cvec, w_mod, b_mod[:, None, :])


def _mod_rows(m_ref, is_ctx, k):
    lo, hi = k * D_MODEL, (k + 1) * D_MODEL
    return jnp.where(is_ctx, m_ref[1:2, lo:hi], m_ref[0:1, lo:hi])


def _rope(x, cos, sin_signed, even):
    partner = jnp.where(even, pltpu.roll(x, LANE - 1, 1), pltpu.roll(x, 1, 1))
    return x * cos + partner * sin_signed


def _stream_specs(stream, tm, n_ctx_tiles):
    xc, xl, lat_first = stream
    ctx_map = lambda i: (jnp.minimum(i, n_ctx_tiles - 1), 0)
    lat_map = lambda i: (jnp.maximum(i - n_ctx_tiles, 0) + lat_first, 0)
    return (xc, xl), [pl.BlockSpec((tm, D_MODEL), ctx_map), pl.BlockSpec((tm, D_MODEL), lat_map)]


def _inproj_kernel(n_ctx_tiles, xc_ref, xl_ref, m_ref, g_ref, w_ref, qg_ref, kg_ref, w2_ref, b2_ref, cos_ref, sin_ref,
                   q_ref, k_ref, v_ref, gq_ref, gk_ref, gv_ref, r_ref, la_ref):
    is_ctx = pl.program_id(0) < n_ctx_tiles
    x = jnp.where(is_ctx, xc_ref[...], xl_ref[...])
    h = _rms(x, g_ref[...]) * (1.0 + _mod_rows(m_ref, is_ctx, 1)) + _mod_rows(m_ref, is_ctx, 0)
    hb = h.astype(BF16)
    cos = cos_ref[...]
    sin = sin_ref[...]
    even = (lax.broadcasted_iota(jnp.int32, cos.shape, 1) % 2) == 0
    q_scale = HEAD_DIM ** -0.5 * math.log2(math.e)

    def head(p, gain, scale):
        y = _rope(_rms(p, gain), cos, sin, even)
        return (y * scale).astype(BF16) if scale != 1.0 else y.astype(BF16)

    pq = _dot(hb, w_ref[:, 0:ATTN_WIDTH])
    for hh in range(ATTN_HEADS):
        sl = slice(hh * HEAD_DIM, (hh + 1) * HEAD_DIM)
        q_ref[:, sl] = head(pq[:, sl], qg_ref[...], q_scale)
    c0 = ATTN_WIDTH
    pk = _dot(hb, w_ref[:, c0:c0 + KV_WIDTH])
    for hh in range(ATTN_KV_HEADS):
        sl = slice(hh * HEAD_DIM, (hh + 1) * HEAD_DIM)
        k_ref[:, sl] = head(pk[:, sl], kg_ref[...], 1.0)
    c0 += KV_WIDTH
    v_ref[...] = _dot(hb, w_ref[:, c0:c0 + KV_WIDTH]).astype(BF16)
    c0 += KV_WIDTH
    gq_ref[...] = _dot(hb, w_ref[:, c0:c0 + GLA_KEY_WIDTH]) * (GLA_DK ** -0.5)
    c0 += GLA_KEY_WIDTH
    gk_ref[...] = _dot(hb, w_ref[:, c0:c0 + GLA_KEY_WIDTH])
    c0 += GLA_KEY_WIDTH
    gv_ref[...] = _dot(hb, w_ref[:, c0:c0 + GLA_VALUE_WIDTH]).astype(BF16)
    c0 += GLA_VALUE_WIDTH
    r_ref[...] = _dot(hb, w_ref[:, c0:c0 + GLA_VALUE_WIDTH])
    a = _dot(hb, w_ref[:, GATE_COL:GATE_COL + LANE])
    z = _dot_split(a, w2_ref[...]) + b2_ref[...]
    la_ref[...] = (jnp.minimum(z, 0.0) - jnp.log(1.0 + jnp.exp(-jnp.abs(z)))) * (1.0 / GLA_GATE_TEMP)


def _inproj(stream, rows, mods_l, norm_g, w_in_b, q_gain, k_gain, w2, b2, cos_t, sin_t, tm, n_ctx_tiles):
    row = lambda i: (i, 0)
    const = lambda i: (0, 0)
    xs, x_specs = _stream_specs(stream, tm, n_ctx_tiles)
    out_shapes = (
        jax.ShapeDtypeStruct((rows, ATTN_WIDTH), BF16),
        jax.ShapeDtypeStruct((rows, KV_WIDTH), BF16),
        jax.ShapeDtypeStruct((rows, KV_WIDTH), BF16),
        jax.ShapeDtypeStruct((rows, GLA_KEY_WIDTH), F32),
        jax.ShapeDtypeStruct((rows, GLA_KEY_WIDTH), F32),
        jax.ShapeDtypeStruct((rows, GLA_VALUE_WIDTH), BF16),
        jax.ShapeDtypeStruct((rows, GLA_VALUE_WIDTH), F32),
        jax.ShapeDtypeStruct((rows, 2 * GLA_KEY_WIDTH), F32),
    )
    return pl.pallas_call(
        functools.partial(_inproj_kernel, n_ctx_tiles),
        grid=(rows // tm,),
        in_specs=x_specs + [
            pl.BlockSpec(mods_l.shape, const),
            pl.BlockSpec((1, D_MODEL), const),
            pl.BlockSpec(w_in_b.shape, const, pipeline_mode=pl.Buffered(1)),
            pl.BlockSpec((1, HEAD_DIM), const),
            pl.BlockSpec((1, HEAD_DIM), const),
            pl.BlockSpec(w2.shape, const),
            pl.BlockSpec(b2.shape, const),
            pl.BlockSpec((tm, LANE), row),
            pl.BlockSpec((tm, LANE), row),
        ],
        out_specs=tuple(pl.BlockSpec((tm, s.shape[1]), row) for s in out_shapes),
        out_shape=out_shapes,
        compiler_params=_cparams(("arbitrary",)),
        name="inproj",
    )(*xs, mods_l, norm_g, w_in_b, q_gain, k_gain, w2, b2, cos_t, sin_t)


def _with_ones(v):
    return jnp.concatenate([v, jnp.ones_like(v)], axis=1)


def _attn_kernel(n_ctx_qtiles, n_ctx, kv_tile, n_pairs, q_ref, k_ref, v_ref, o_ref, s0_ref, s1_ref, m_ref, acc_ref):
    tq = q_ref.shape[0]
    q = jnp.concatenate([q_ref[:, g * HEAD_DIM:(g + 1) * HEAD_DIM] for g in range(ATTN_GROUP)], axis=0)
    is_ctx = pl.program_id(1) < n_ctx_qtiles

    def finish(acc):
        o = acc[:, :HEAD_DIM] / acc[:, HEAD_DIM:]
        o_ref[...] = jnp.concatenate([o[g * tq:(g + 1) * tq, :] for g in range(ATTN_GROUP)], axis=1).astype(BF16)

    @pl.when(is_ctx)
    def _():
        s = _dot_nt(q, k_ref[0:n_ctx, :])
        p = jnp.exp2(s - jnp.max(s, axis=1, keepdims=True)).astype(BF16)
        finish(_dot(p, _with_ones(v_ref[0:n_ctx, :])))

    @pl.when(jnp.logical_not(is_ctx))
    def _():
        reps = kv_tile // LANE
        m_ref[...] = jnp.full(m_ref.shape, -jnp.inf, F32)
        acc_ref[...] = jnp.zeros(acc_ref.shape, F32)

        def scores(tile):
            start = pl.multiple_of(tile * kv_tile, LANE)
            return _dot_nt(q, k_ref[pl.ds(start, kv_tile), :])

        def update(s_ref, tile):
            start = pl.multiple_of(tile * kv_tile, LANE)
            s = s_ref[...]
            m_prev = m_ref[...]
            m_new = jnp.maximum(m_prev, jnp.max(s, axis=1, keepdims=True))
            alpha = jnp.exp2(m_prev - m_new)
            p = jnp.exp2(s - jnp.tile(m_new, (1, reps))).astype(BF16)
            acc_ref[...] = jnp.tile(alpha, (1, 2)) * acc_ref[...] + _dot(p, _with_ones(v_ref[pl.ds(start, kv_tile), :]))
            m_ref[...] = m_new

        s0_ref[...] = scores(0)

        def pair(jj, carry):
            s1_ref[...] = scores(2 * jj + 1)
            update(s0_ref, 2 * jj)
            s0_ref[...] = scores(jnp.minimum(2 * jj + 2, 2 * n_pairs - 1))
            update(s1_ref, 2 * jj + 1)
            return carry

        lax.fori_loop(0, n_pairs, pair, 0)
        finish(acc_ref[...])


def _attention(q, k, v, n_ctx, tq, kv_tile):
    rows = q.shape[0]
    gw = ATTN_GROUP * HEAD_DIM
    n_tiles = rows // kv_tile
    score = pltpu.VMEM((ATTN_GROUP * tq, kv_tile), F32)
    return pl.pallas_call(
        functools.partial(_attn_kernel, n_ctx // tq, n_ctx, kv_tile, n_tiles // 2),
        grid=(ATTN_KV_HEADS, rows // tq),
        in_specs=[
            pl.BlockSpec((tq, gw), lambda g, i: (i, g)),
            pl.BlockSpec((rows, HEAD_DIM), lambda g, i: (0, g)),
            pl.BlockSpec((rows, HEAD_DIM), lambda g, i: (0, g)),
        ],
        out_specs=pl.BlockSpec((tq, gw), lambda g, i: (i, g)),
        out_shape=jax.ShapeDtypeStruct((rows, ATTN_WIDTH), BF16),
        scratch_shapes=[score, score, pltpu.VMEM((ATTN_GROUP * tq, LANE), F32),
                        pltpu.VMEM((ATTN_GROUP * tq, 2 * HEAD_DIM), F32)],
        compiler_params=_cparams(("arbitrary", "arbitrary")),
        name="attention",
    )(q, k, v)


def _kv_tile(rows):
    for n in range(1536 // LANE, 0, -1):
        if rows % (n * LANE) == 0 and (rows // (n * LANE)) % 2 == 0:
            return n * LANE
    raise ValueError("unsupported token count for the attention key tiling")


def _gla_direction(reverse, q_ref, k_ref, v_ref, la_ref, o_ref, s_ref):
    n = GLA_CHUNK
    row = lax.broadcasted_iota(jnp.int32, (n, n), 0)
    col = lax.broadcasted_iota(jnp.int32, (n, n), 1)
    tri = jnp.where((col >= row) if reverse else (col <= row), 1.0, 0.0).astype(BF16)
    valid_pair = (col >= row) if reverse else (col <= row)
    rows1 = lax.broadcasted_iota(jnp.int32, (n, 1), 0)
    last = 0 if reverse else n - 1
    n_sub = q_ref.shape[0] // n
    b_all = {}
    for c in range(n_sub):
        pieces = _split3(la_ref[c * n:(c + 1) * n, :])
        b_all[c] = _dot(tri, pieces[0]) + _dot(tri, pieces[1]) + _dot(tri, pieces[2])
    for c, h in [(c, h) for c in (range(n_sub - 1, -1, -1) if reverse else range(n_sub)) for h in range(GLA_HEADS)]:
        rs = slice(c * n, (c + 1) * n)
        ks = slice(h * GLA_DK, (h + 1) * GLA_DK)
        q = q_ref[rs, ks]
        k = k_ref[rs, ks]
        v = v_ref[rs, h * GLA_DV:(h + 1) * GLA_DV]
        b = b_all[c][:, ks]
        b_tot = b[last:last + 1, :]
        a_rows = []
        for blk in range(n // GLA_SUB):
            lo, hi = blk * GLA_SUB, (blk + 1) * GLA_SUB
            ref_row = hi - 1 if reverse else lo
            b_ref = b[ref_row:ref_row + 1, :]
            q_blk = q[lo:hi, :] * jnp.exp(b[lo:hi, :] - b_ref)
            reach = (rows1 >= lo) if reverse else (rows1 < hi)
            k_blk = jnp.where(reach, k * jnp.exp(jnp.where(reach, b_ref - b, 0.0)), 0.0)
            a_rows.append(_dot_nt(q_blk.astype(BF16), k_blk.astype(BF16)))
        a = jnp.where(valid_pair, jnp.concatenate(a_rows, axis=0), 0.0).astype(BF16)
        st = s_ref[h]
        o = _dot(a, v) + _dot_nt((q * jnp.exp(b)).astype(BF16), st.astype(BF16))
        o_ref[rs, h * GLA_DV:(h + 1) * GLA_DV] = o
        k_end = (k * jnp.exp(b_tot - b)).astype(BF16)
        s_ref[h] = jnp.exp(b_tot) * st + _dot_tn(v, k_end)


def _gla_kernel(qf, kf, vf, laf, qb, kb, vb, lab, of_ref, ob_ref, sf_ref, sb_ref):
    @pl.when(pl.program_id(0) == 0)
    def _():
        sf_ref[...] = jnp.zeros(sf_ref.shape, F32)
        sb_ref[...] = jnp.zeros(sb_ref.shape, F32)

    _gla_direction(False, qf, kf, vf, laf, of_ref, sf_ref)
    _gla_direction(True, qb, kb, vb, lab, ob_ref, sb_ref)


def _gla(gq, gk, gv, la, n_ctx):
    rows = gq.shape[0]
    step = next(m * GLA_CHUNK for m in (4, 2, 1) if n_ctx % (m * GLA_CHUNK) == 0 and rows % (m * GLA_CHUNK) == 0)
    n_chunks = rows // step
    n_ctx_chunks = n_ctx // step

    def bchunk(s):
        return jnp.where(s < n_ctx_chunks, n_ctx_chunks - 1 - s, n_chunks - 1 - s + n_ctx_chunks)

    fwd = lambda s: (s, 0)
    bwd = lambda s: (bchunk(s), 0)
    bwd_la = lambda s: (bchunk(s), 1)
    key_blk = (step, GLA_KEY_WIDTH)
    val_blk = (step, GLA_VALUE_WIDTH)
    state = pltpu.VMEM((GLA_HEADS, GLA_DV, GLA_DK), F32)
    return pl.pallas_call(
        _gla_kernel,
        grid=(n_chunks,),
        in_specs=[
            pl.BlockSpec(key_blk, fwd), pl.BlockSpec(key_blk, fwd), pl.BlockSpec(val_blk, fwd),
            pl.BlockSpec(key_blk, fwd),
            pl.BlockSpec(key_blk, bwd), pl.BlockSpec(key_blk, bwd), pl.BlockSpec(val_blk, bwd),
            pl.BlockSpec(key_blk, bwd_la),
        ],
        out_specs=(pl.BlockSpec(val_blk, fwd), pl.BlockSpec(val_blk, bwd)),
        out_shape=(jax.ShapeDtypeStruct((rows, GLA_VALUE_WIDTH), F32),) * 2,
        scratch_shapes=[state, state],
        compiler_params=_cparams(("arbitrary",)),
        name="gla_scan",
    )(gq, gk, gv, la, gq, gk, gv, la)


def _outproj_kernel(n_ctx_tiles, xc_ref, xl_ref, a_ref, of_ref, ob_ref, r_ref, m_ref, gg_ref, w_ref, nf_ref,
                    wrh_ref, wrl_ref, xo_ref, h2_ref, aff_ref):
    is_ctx = pl.program_id(0) < n_ctx_tiles
    gla = of_ref[...] + ob_ref[...]
    r = r_ref[...]
    parts = [a_ref[...]]
    for h in range(GLA_HEADS):
        sl = slice(h * GLA_DV, (h + 1) * GLA_DV)
        parts.append((_rms(gla[:, sl], gg_ref[...]) * _silu(r[:, sl])).astype(BF16))
    y = _dot(jnp.concatenate(parts, axis=1), w_ref[...])
    x = jnp.where(is_ctx, xc_ref[...], xl_ref[...]) + _mod_rows(m_ref, is_ctx, 2) * y
    xo_ref[...] = x
    h2 = _rms(x, nf_ref[...]) * (1.0 + _mod_rows(m_ref, is_ctx, 4)) + _mod_rows(m_ref, is_ctx, 3)
    h2_ref[...] = h2
    h_hi, h_lo = _split2(h2)
    logits = _dot(h_hi, wrh_ref[...]) + _dot(h_lo, wrh_ref[...]) + _dot(h_hi, wrl_ref[...])
    live = lax.broadcasted_iota(jnp.int32, logits.shape, 1) < N_EXPERTS
    logits = jnp.where(live, logits, -jnp.inf)
    e = jnp.exp(logits - jnp.max(logits, axis=-1, keepdims=True))
    aff_ref[...] = e / jnp.sum(e, axis=-1, keepdims=True)


def _outproj(stream, rows, attn_o, o_f, o_b, r, mods_l, gla_gain, w_out_b, norm_ffn, wr_hi, wr_lo, tm, n_ctx_tiles):
    row = lambda i: (i, 0)
    const = lambda i: (0, 0)
    xs, x_specs = _stream_specs(stream, tm, n_ctx_tiles)
    return pl.pallas_call(
        functools.partial(_outproj_kernel, n_ctx_tiles),
        grid=(rows // tm,),
        in_specs=x_specs + [
            pl.BlockSpec((tm, ATTN_WIDTH), row),
            pl.BlockSpec((tm, GLA_VALUE_WIDTH), row),
            pl.BlockSpec((tm, GLA_VALUE_WIDTH), row),
            pl.BlockSpec((tm, GLA_VALUE_WIDTH), row),
            pl.BlockSpec(mods_l.shape, const),
            pl.BlockSpec((1, GLA_DV), const),
            pl.BlockSpec(w_out_b.shape, const, pipeline_mode=pl.Buffered(1)),
            pl.BlockSpec((1, D_MODEL), const),
            pl.BlockSpec(wr_hi.shape, const),
            pl.BlockSpec(wr_lo.shape, const),
        ],
        out_specs=(
            pl.BlockSpec((tm, D_MODEL), row),
            pl.BlockSpec((tm, D_MODEL), row),
            pl.BlockSpec((tm, LANE), row),
        ),
        out_shape=(
            jax.ShapeDtypeStruct((rows, D_MODEL), F32),
            jax.ShapeDtypeStruct((rows, D_MODEL), F32),
            jax.ShapeDtypeStruct((rows, LANE), F32),
        ),
        compiler_params=_cparams(("arbitrary",)),
        name="outproj",
    )(*xs, attn_o, o_f, o_b, r, mods_l, gla_gain, w_out_b, norm_ffn, wr_hi, wr_lo)


def _exclusive_rank(mask_b):
    nr = mask_b.shape[0]
    li = lax.broadcasted_iota(jnp.int32, (LANE, LANE), 0)
    lj = lax.broadcasted_iota(jnp.int32, (LANE, LANE), 1)
    within = _dot(mask_b, jnp.where(li < lj, 1.0, 0.0).astype(BF16))
    ri = lax.broadcasted_iota(jnp.int32, (nr, nr), 0)
    rj = lax.broadcasted_iota(jnp.int32, (nr, nr), 1)
    before = jnp.sum(_dot(jnp.where(rj < ri, 1.0, 0.0).astype(BF16), mask_b), axis=1, keepdims=True)
    return within + before, before


SEARCH_STEPS = 127
COMPACT_WINDOW = LANE + 16


def _select_kernel(cap, aff_ref, pos_ref, off_ref):
    a = aff_ref[...]
    ne = a.shape[0]

    def count(mask):
        return jnp.sum(jnp.sum(jnp.where(mask, 1.0, 0.0), axis=2, keepdims=True), axis=1, keepdims=True)

    def search(_, carry):
        t, step = carry
        cand = t + step
        return jnp.where(count(a >= cand) >= cap, cand, t), step * 0.5

    thr, _ = lax.fori_loop(0, SEARCH_STEPS, search, (jnp.zeros((ne, 1, 1), F32), jnp.ones((1, 1, 1), F32)))
    room = cap - count(a > thr)
    for e in range(ne):
        above = a[e] > thr[e]
        tie = a[e] == thr[e]
        tie_rank, _ = _exclusive_rank(jnp.where(tie, 1.0, 0.0).astype(BF16))
        chosen = above | (tie & (tie_rank < room[e]))
        slot, before = _exclusive_rank(jnp.where(chosen, 1.0, 0.0).astype(BF16))
        pos_ref[e] = jnp.where(chosen, slot, -1.0)
        off_ref[e] = jnp.broadcast_to(before, slot.shape)


def _compact_kernel(cap, off_ref, pos_ref, aff_ref, out_ref, acc_ref):
    e = pl.program_id(0)
    nr = pos_ref.shape[1]
    acc_ref[...] = jnp.zeros(acc_ref.shape, F32)
    rel_ids = lax.broadcasted_iota(jnp.int32, (COMPACT_WINDOW, LANE), 0).astype(F32)
    lane_row = lax.broadcasted_iota(jnp.int32, (1, LANE), 1).astype(F32)
    pad_rows = jnp.zeros((LANE - SUBLANE, LANE), BF16)

    def gather_row(r, carry):
        start = pl.multiple_of(jnp.minimum((off_ref[e, r] // SUBLANE) * SUBLANE, cap), SUBLANE)
        slot_ids = rel_ids + start.astype(F32)
        onehot = jnp.where(slot_ids == pos_ref[0, pl.ds(r, 1), :], 1.0, 0.0).astype(BF16)
        a_hi, a_mid, a_lo = _split3(aff_ref[0, pl.ds(r, 1), :])
        r_row = jnp.full((1, LANE), r, jnp.int32).astype(F32)
        vals = jnp.concatenate(
            [lane_row.astype(BF16), r_row.astype(BF16), a_hi, a_mid, a_lo, jnp.zeros((3, LANE), BF16), pad_rows],
            axis=0)
        acc_ref[pl.ds(start, COMPACT_WINDOW), :] += _dot_nt(onehot, vals)
        return carry

    lax.fori_loop(0, nr, gather_row, 0, unroll=4)
    acc = acc_ref[0:cap, :]
    token = acc[:, 0:1] + LANE * acc[:, 1:2]
    gate = acc[:, 2:3] + acc[:, 3:4] + acc[:, 4:5]
    lane = lax.broadcasted_iota(jnp.int32, (cap, LANE), 1)
    out_ref[0] = jnp.where(lane == 0, token, jnp.where(lane == 1, gate, 0.0))


def _route(aff_t, cap):
    ne, nr, _ = aff_t.shape
    assert nr <= 256 and nr % SUBLANE == 0 and cap % SUBLANE == 0
    whole = pl.BlockSpec((ne, nr, LANE), lambda i: (0, 0, 0))
    pos, off = pl.pallas_call(
        functools.partial(_select_kernel, cap),
        grid=(1,),
        in_specs=[whole],
        out_specs=(whole, whole),
        out_shape=(jax.ShapeDtypeStruct(aff_t.shape, F32),) * 2,
        compiler_params=_cparams(("arbitrary",)),
        name="route_select",
    )(aff_t)
    per_expert = pl.BlockSpec((1, nr, LANE), lambda e, off: (e, 0, 0))
    return pl.pallas_call(
        functools.partial(_compact_kernel, cap),
        grid_spec=pltpu.PrefetchScalarGridSpec(
            num_scalar_prefetch=1,
            grid=(ne,),
            in_specs=[per_expert, per_expert],
            out_specs=pl.BlockSpec((1, cap, LANE), lambda e, off: (e, 0, 0)),
            scratch_shapes=[pltpu.VMEM((cap + COMPACT_WINDOW, LANE), F32)],
        ),
        out_shape=jax.ShapeDtypeStruct((ne, cap, LANE), F32),
        compiler_params=_cparams(("arbitrary",)),
        name="route_compact",
    )(off[:, :, 0].astype(jnp.int32), pos, aff_t)


def _route_tokens(aff, lo, n, offset):
    cap = CAPACITY_FACTOR * n // N_EXPERTS
    a = aff[lo:lo + n, :N_EXPERTS].T.reshape(N_EXPERTS, n // LANE, LANE)
    pad = (-a.shape[1]) % SUBLANE
    if pad:
        a = jnp.concatenate([a, jnp.full((N_EXPERTS, pad, LANE), -1.0, F32)], axis=1)
    sel = _route(a, cap)
    return sel[:, :, 0].astype(jnp.int32) + offset, sel[:, :, 1]


def _moe_kernel(n_lat, nf, idx_ref, gate_ref, m_ref, wg_ref, wu_ref, wd_ref, h2_hbm, x_hbm, xo_hbm,
                stage_h, stage_x, xg, acc, sem_h, sem_in, sem_out):
    del x_hbm
    e = pl.program_id(0)
    f = pl.program_id(1)
    ne = pl.num_programs(0)
    groups = stage_h.shape[0]
    ns = groups * SUBLANE
    per_step = ns // nf
    down_cols = 4 * LANE

    def row_copy(t, g, i, to_vmem, hbm, stage, sem):
        if to_vmem:
            return pltpu.make_async_copy(hbm.at[pl.ds(t, 1), :], stage.at[g, pl.ds(i, 1), :], sem)
        return pltpu.make_async_copy(stage.at[g, pl.ds(i, 1), :], hbm.at[pl.ds(t, 1), :], sem)

    def start_all_rows(expert, to_vmem, hbm, stage, sem):
        def group(g, carry):
            first = expert * ns + g * SUBLANE
            for i in range(SUBLANE):
                row_copy(idx_ref[first + i], g, i, to_vmem, hbm, stage, sem).start()
            return carry
        lax.fori_loop(0, groups, group, 0)

    def start_step_rows(expert, to_vmem, hbm, stage, sem):
        first = expert * ns + f * per_step
        for j in range(per_step):
            if per_step % SUBLANE == 0:
                g, i = f * (per_step // SUBLANE) + j // SUBLANE, j % SUBLANE
            else:
                s = f * per_step + j
                g, i = s // SUBLANE, s % SUBLANE
            row_copy(idx_ref[first + j], g, i, to_vmem, hbm, stage, sem).start()

    def wait_rows(stage, sem):
        pltpu.make_async_copy(stage, stage, sem).wait()

    @pl.when(f == 0)
    def _():
        @pl.when(e == 0)
        def _():
            start_all_rows(0, True, h2_hbm, stage_h, sem_h)

        wait_rows(stage_h, sem_h)
        xg[...] = stage_h[...].reshape(ns, D_MODEL).astype(BF16)
        acc[...] = jnp.zeros(acc.shape, F32)

    @pl.when(f == nf - 2)
    def _():
        @pl.when(e > 0)
        def _():
            wait_rows(stage_x, sem_out)

        start_all_rows(e, True, xo_hbm, stage_x, sem_in)

    start_step_rows(jnp.minimum(e + 1, ne - 1), True, h2_hbm, stage_h, sem_h)
    x = xg[...]
    hid = (_silu(_dot(x, wg_ref[0, 0].astype(BF16))) * _dot(x, wu_ref[0, 0].astype(BF16))).astype(BF16)
    for n in range(D_MODEL // down_cols):
        cols = slice(n * down_cols, (n + 1) * down_cols)
        part = _dot(hid, wd_ref[0, 0, :, cols].astype(BF16))
        acc[:, cols] += part

    @pl.when(f == nf - 1)
    def _():
        is_lat = lax.broadcasted_iota(jnp.int32, (ns, 1), 0) < n_lat
        gate = gate_ref[0]
        wait_rows(stage_x, sem_in)
        for n in range(D_MODEL // down_cols):
            cols = slice(n * down_cols, (n + 1) * down_cols)
            mcols = slice(5 * D_MODEL + n * down_cols, 5 * D_MODEL + (n + 1) * down_cols)
            scale = gate * jnp.where(is_lat, m_ref[0:1, mcols], m_ref[1:2, mcols])
            y = (acc[:, cols] * scale).reshape(groups, SUBLANE, down_cols)
            stage_x[:, :, cols] = stage_x[:, :, cols] + y
        start_all_rows(e, False, xo_hbm, stage_x, sem_out)

        @pl.when(e == ne - 1)
        def _():
            wait_rows(stage_x, sem_out)
            wait_rows(stage_h, sem_h)


def _moe(layer, idx, gate, mods_l, w_gate, w_up, w_down, h2, x2, n_lat):
    ne, ns = idx.shape
    ft = 256 if ns % (EXPERT_FF // 256) == 0 else 384
    nf = EXPERT_FF // ft
    assert nf >= 2 and ns % nf == 0 and ns % SUBLANE == 0
    stage = pltpu.VMEM((ns // SUBLANE, SUBLANE, D_MODEL), F32)
    grid_spec = pltpu.PrefetchScalarGridSpec(
        num_scalar_prefetch=1,
        grid=(ne, nf),
        in_specs=[
            pl.BlockSpec((1, ns, 1), lambda e, f, idx: (e, 0, 0)),
            pl.BlockSpec(mods_l.shape, lambda e, f, idx: (0, 0)),
            pl.BlockSpec((1, 1, D_MODEL, ft), lambda e, f, idx: (layer, e, 0, f)),
            pl.BlockSpec((1, 1, D_MODEL, ft), lambda e, f, idx: (layer, e, 0, f)),
            pl.BlockSpec((1, 1, ft, D_MODEL), lambda e, f, idx: (layer, e, f, 0)),
            pl.BlockSpec(memory_space=pl.ANY),
            pl.BlockSpec(memory_space=pl.ANY),
        ],
        out_specs=pl.BlockSpec(memory_space=pl.ANY),
        scratch_shapes=[
            stage,
            stage,
            pltpu.VMEM((ns, D_MODEL), BF16),
            pltpu.VMEM((ns, D_MODEL), F32),
            pltpu.SemaphoreType.DMA,
            pltpu.SemaphoreType.DMA,
            pltpu.SemaphoreType.DMA,
        ],
    )
    return pl.pallas_call(
        functools.partial(_moe_kernel, n_lat, nf),
        grid_spec=grid_spec,
        out_shape=jax.ShapeDtypeStruct(x2.shape, F32),
        input_output_aliases={7: 0},
        compiler_params=_cparams(("arbitrary", "arbitrary")),
        name="moe",
    )(idx.reshape(-1), gate[:, :, None], mods_l, w_gate, w_up, w_down, h2, x2)


def _final_kernel(x_ref, g_ref, o_ref):
    o_ref[...] = _rms(x_ref[...], g_ref[...])


def _final_norm(x2, gain, n_ctx, tm):
    n_lat = x2.shape[0] - n_ctx
    skip = n_ctx // tm
    return pl.pallas_call(
        _final_kernel,
        grid=(n_lat // tm,),
        in_specs=[
            pl.BlockSpec((tm, D_MODEL), lambda i: (i + skip, 0)),
            pl.BlockSpec((1, D_MODEL), lambda i: (0, 0)),
        ],
        out_specs=pl.BlockSpec((tm, D_MODEL), lambda i: (i, 0)),
        out_shape=jax.ShapeDtypeStruct((n_lat, D_MODEL), F32),
        compiler_params=_cparams(("arbitrary",)),
        name="final_norm",
    )(x2, gain)


def _rope_tables(n_lat, n_ctx):
    rows = n_lat // GRID_W
    row = jnp.repeat(jnp.arange(rows, dtype=F32), GRID_W)
    col = jnp.tile(jnp.arange(GRID_W, dtype=F32), rows)
    half = HEAD_DIM // 2
    inv_freq = ROPE_THETA ** (-jnp.arange(0, half, 2, dtype=F32) / half)
    ang = jnp.concatenate([row[:, None] * inv_freq, col[:, None] * inv_freq], axis=-1)
    cos = jnp.repeat(jnp.cos(ang), 2, axis=-1)
    sign = jnp.tile(jnp.array([-1.0, 1.0], F32), half)
    sin = jnp.repeat(jnp.sin(ang), 2, axis=-1) * sign
    cos = jnp.concatenate([jnp.ones((n_ctx, HEAD_DIM), F32), cos], axis=0)
    sin = jnp.concatenate([jnp.zeros((n_ctx, HEAD_DIM), F32), sin], axis=0)
    return cos, sin


def _row_tile(n_lat, n_ctx):
    for tm in (256, 128, 64):
        if n_lat % tm == 0 and n_ctx % tm == 0:
            return tm
    raise ValueError("token counts must be multiples of 64")


def kernel(x, c, ctx, c_ctx, w_mod, b_mod, norm_mix, w_in, q_gain, k_gain, w_gla_a2, b_gla_a, gla_gain,
           w_out, norm_ffn, w_router, w_gate, w_up, w_down, final_norm):
    batch, n_lat, _ = x.shape
    n_ctx = ctx.shape[1]
    depth = w_mod.shape[0]
    assert batch == 1 and c.shape[0] == 1
    tm = _row_tile(n_lat, n_ctx)
    n_ctx_tiles = n_ctx // tm
    tq = next(t for t in (256, 128, 64) if n_ctx % t == 0)
    kv_tile = _kv_tile(n_ctx + n_lat)

    cvec = jnp.concatenate([c, c_ctx[None, :], jnp.zeros((SUBLANE - 2, D_MODEL), F32)], axis=0)
    mods = _modulation(cvec, w_mod, b_mod)
    cos_t, sin_t = _rope_tables(n_lat, n_ctx)
    rows = n_ctx + n_lat
    stream = (ctx[0], x[0], 0)

    for l in range(depth):
        last = l == depth - 1
        w_in_b = jnp.pad(w_in[l], ((0, 0), (0, IN_WIDTH_PAD - IN_WIDTH))).astype(BF16)
        w2 = jnp.zeros((LANE, 2 * GLA_KEY_WIDTH), F32)
        o0 = IN_WIDTH - 2 * GLA_GATE_RANK - GATE_COL
        w2 = w2.at[o0:o0 + GLA_GATE_RANK, :GLA_KEY_WIDTH].set(w_gla_a2[l, 0])
        w2 = w2.at[o0 + GLA_GATE_RANK:o0 + 2 * GLA_GATE_RANK, GLA_KEY_WIDTH:].set(w_gla_a2[l, 1])
        b2 = b_gla_a[l].reshape(1, 2 * GLA_KEY_WIDTH)
        q, k, v, gq, gk, gv, r, la = _inproj(
            stream, rows, mods[l], norm_mix[l][None, :], w_in_b, q_gain[l][None, :], k_gain[l][None, :], w2, b2,
            cos_t, sin_t, tm, n_ctx_tiles)
        attn_o = _attention(q, k, v, n_ctx, tq, kv_tile)
        o_f, o_b = _gla(gq, gk, gv, la, n_ctx)
        wr = jnp.pad(w_router[l], ((0, 0), (0, LANE - N_EXPERTS)))
        wr_hi, wr_lo = _split2(wr)
        x2, h2, aff = _outproj(stream, rows, attn_o, o_f, o_b, r, mods[l], gla_gain[l][None, :],
                               w_out[l].astype(BF16), norm_ffn[l][None, :], wr_hi, wr_lo, tm, n_ctx_tiles)
        idx, gate = _route_tokens(aff, n_ctx, n_lat, n_ctx)
        n_lat_slots = idx.shape[1]
        if not last:
            idx_c, gate_c = _route_tokens(aff, 0, n_ctx, 0)
            idx = jnp.concatenate([idx, idx_c], axis=1)
            gate = jnp.concatenate([gate, gate_c], axis=1)
        x2 = _moe(l, idx, gate, mods[l], w_gate, w_up, w_down, h2, x2, n_lat_slots)
        stream = (x2, x2, n_ctx_tiles)

    return _final_norm(x2, final_norm[None, :], n_ctx, tm)[None]
```

```python
import functools
import math

import jax
import jax.numpy as jnp
from jax import lax
from jax.experimental import pallas as pl
from jax.experimental.pallas import tpu as pltpu

D_MODEL = 2048
HEAD_DIM = 128
ATTN_HEADS = 8
ATTN_KV_HEADS = 2
ATTN_GROUP = ATTN_HEADS // ATTN_KV_HEADS
ATTN_WIDTH = ATTN_HEADS * HEAD_DIM
KV_WIDTH = ATTN_KV_HEADS * HEAD_DIM
GLA_HEADS = 4
GLA_DK = 128
GLA_DV = 256
GLA_KEY_WIDTH = GLA_HEADS * GLA_DK
GLA_VALUE_WIDTH = GLA_HEADS * GLA_DV
GLA_GATE_RANK = 16
GLA_GATE_TEMP = 16.0
GLA_CHUNK = 64
GLA_SUB = 16
N_EXPERTS = 16
EXPERT_FF = 1536
CAPACITY_FACTOR = 2
GRID_W = 64
ROPE_THETA = 10000.0
EPS = 1e-6
IN_WIDTH = ATTN_WIDTH + 2 * KV_WIDTH + 2 * GLA_KEY_WIDTH + 2 * GLA_VALUE_WIDTH + 2 * GLA_GATE_RANK

LANE = 128
SUBLANE = 8
VMEM_LIMIT_BYTES = 56 * 1024 * 1024

IN_WIDTH_PAD = ((IN_WIDTH + LANE - 1) // LANE) * LANE
GATE_COL = IN_WIDTH_PAD - LANE

F32 = jnp.float32
BF16 = jnp.bfloat16


def _cparams(sem):
    return pltpu.CompilerParams(dimension_semantics=sem, vmem_limit_bytes=VMEM_LIMIT_BYTES)


def _dot(a, b):
    return jnp.dot(a, b, preferred_element_type=F32)


def _dot_nt(a, b):
    return lax.dot_general(a, b, (((1,), (1,)), ((), ())), preferred_element_type=F32)


def _dot_tn(a, b):
    return lax.dot_general(a, b, (((0,), (0,)), ((), ())), preferred_element_type=F32)


def _split2(a):
    hi = a.astype(BF16)
    lo = (a - hi.astype(F32)).astype(BF16)
    return hi, lo


def _split3(a):
    hi = a.astype(BF16)
    r1 = a - hi.astype(F32)
    mid = r1.astype(BF16)
    lo = (r1 - mid.astype(F32)).astype(BF16)
    return hi, mid, lo


def _dot_split(a, w):
    a_hi, a_lo = _split2(a)
    w_hi, w_lo = _split2(w)
    return _dot(a_hi, w_hi) + _dot(a_lo, w_hi) + _dot(a_hi, w_lo)


def _sigmoid(x):
    return 1.0 / (1.0 + jnp.exp(-x))


def _silu(x):
    return x * _sigmoid(x)


def _rms(x, gain):
    return x * lax.rsqrt(jnp.mean(x * x, axis=-1, keepdims=True) + EPS) * gain


def _mod_kernel(c_ref, w_ref, b_ref, o_ref):
    o_ref[0] = _dot_split(_silu(c_ref[...]), w_ref[0]) + b_ref[0]


def _modulation(cvec, w_mod, b_mod):
    depth, _, width = w_mod.shape
    tile = 1536
    return pl.pallas_call(
        _mod_kernel,
        grid=(depth, width // tile),
        in_specs=[
            pl.BlockSpec((SUBLANE, D_MODEL), lambda l, j: (0, 0)),
            pl.BlockSpec((1, D_MODEL, tile), lambda l, j: (l, 0, j)),
            pl.BlockSpec((1, 1, tile), lambda l, j: (l, 0, j)),
        ],
        out_specs=pl.BlockSpec((1, SUBLANE, tile), lambda l, j: (l, 0, j)),
        out_shape=jax.ShapeDtypeStruct((depth, SUBLANE, width), F32),
        compiler_params=_cparams(("arbitrary", "arbitrary")),
        name="modulation",
    )(cvec, w_mod, b_mod[:, None, :])


def _mod_rows(m_ref, is_ctx, k):
    lo, hi = k * D_MODEL, (k + 1) * D_MODEL
    return jnp.where(is_ctx, m_ref[1:2, lo:hi], m_ref[0:1, lo:hi])


def _rope(x, cos, sin_signed, even):
    partner = jnp.where(even, pltpu.roll(x, LANE - 1, 1), pltpu.roll(x, 1, 1))
    return x * cos + partner * sin_signed


def _stream_specs(stream, tm, n_ctx_tiles):
    xc, xl, lat_first = stream
    ctx_map = lambda i: (jnp.minimum(i, n_ctx_tiles - 1), 0)
    lat_map = lambda i: (jnp.maximum(i - n_ctx_tiles, 0) + lat_first, 0)
    return (xc, xl), [pl.BlockSpec((tm, D_MODEL), ctx_map), pl.BlockSpec((tm, D_MODEL), lat_map)]


def _inproj_kernel(n_ctx_tiles, xc_ref, xl_ref, m_ref, g_ref, w_ref, qg_ref, kg_ref, w2_ref, b2_ref, cos_ref, sin_ref,
                   q_ref, k_ref, v_ref, gq_ref, gk_ref, gv_ref, r_ref, la_ref):
    is_ctx = pl.program_id(0) < n_ctx_tiles
    x = jnp.where(is_ctx, xc_ref[...], xl_ref[...])
    h = _rms(x, g_ref[...]) * (1.0 + _mod_rows(m_ref, is_ctx, 1)) + _mod_rows(m_ref, is_ctx, 0)
    hb = h.astype(BF16)
    cos = cos_ref[...]
    sin = sin_ref[...]
    even = (lax.broadcasted_iota(jnp.int32, cos.shape, 1) % 2) == 0
    q_scale = HEAD_DIM ** -0.5 * math.log2(math.e)

    def head(p, gain, scale):
        y = _rope(_rms(p, gain), cos, sin, even)
        return (y * scale).astype(BF16) if scale != 1.0 else y.astype(BF16)

    pq = _dot(hb, w_ref[0, :,0:ATTN_WIDTH])
    for hh in range(ATTN_HEADS):
        sl = slice(hh * HEAD_DIM, (hh + 1) * HEAD_DIM)
        q_ref[:, sl] = head(pq[:, sl], qg_ref[...], q_scale)
    c0 = ATTN_WIDTH
    pk = _dot(hb, w_ref[0, :,c0:c0 + KV_WIDTH])
    for hh in range(ATTN_KV_HEADS):
        sl = slice(hh * HEAD_DIM, (hh + 1) * HEAD_DIM)
        k_ref[:, sl] = head(pk[:, sl], kg_ref[...], 1.0)
    c0 += KV_WIDTH
    v_ref[...] = _dot(hb, w_ref[0, :,c0:c0 + KV_WIDTH]).astype(BF16)
    c0 += KV_WIDTH
    gq_ref[...] = _dot(hb, w_ref[0, :,c0:c0 + GLA_KEY_WIDTH]) * (GLA_DK ** -0.5)
    c0 += GLA_KEY_WIDTH
    gk_ref[...] = _dot(hb, w_ref[0, :,c0:c0 + GLA_KEY_WIDTH])
    c0 += GLA_KEY_WIDTH
    gv_ref[...] = _dot(hb, w_ref[0, :,c0:c0 + GLA_VALUE_WIDTH]).astype(BF16)
    c0 += GLA_VALUE_WIDTH
    r_ref[...] = _dot(hb, w_ref[0, :,c0:c0 + GLA_VALUE_WIDTH])
    a = _dot(hb, w_ref[0, :,GATE_COL:GATE_COL + LANE])
    z = _dot_split(a, w2_ref[...]) + b2_ref[...]
    la_ref[...] = (jnp.minimum(z, 0.0) - jnp.log(1.0 + jnp.exp(-jnp.abs(z)))) * (1.0 / GLA_GATE_TEMP)


def _cast_pad_kernel(w_ref, o_ref):
    w = w_ref[0].astype(BF16)
    o_ref[0, :, 0:GATE_COL] = w[:, 0:GATE_COL]
    o_ref[0, :, GATE_COL:] = jnp.concatenate(
        [w[:, GATE_COL:], jnp.zeros((w.shape[0], IN_WIDTH_PAD - IN_WIDTH), BF16)], axis=1)


def _cast_pad_in_weights(w_in):
    depth = w_in.shape[0]
    tk = 256
    return pl.pallas_call(
        _cast_pad_kernel,
        grid=(depth, D_MODEL // tk),
        in_specs=[pl.BlockSpec((1, tk, IN_WIDTH), lambda l, k: (l, k, 0))],
        out_specs=pl.BlockSpec((1, tk, IN_WIDTH_PAD), lambda l, k: (l, k, 0)),
        out_shape=jax.ShapeDtypeStruct((depth, D_MODEL, IN_WIDTH_PAD), BF16),
        compiler_params=_cparams(("arbitrary", "arbitrary")),
        name="cast_pad_w_in",
    )(w_in)


def _inproj(layer, stream, rows, mods_l, norm_g, w_in_b, q_gain, k_gain, w2, b2, cos_t, sin_t, tm, n_ctx_tiles):
    row = lambda i: (i, 0)
    const = lambda i: (0, 0)
    xs, x_specs = _stream_specs(stream, tm, n_ctx_tiles)
    out_shapes = (
        jax.ShapeDtypeStruct((rows, ATTN_WIDTH), BF16),
        jax.ShapeDtypeStruct((rows, KV_WIDTH), BF16),
        jax.ShapeDtypeStruct((rows, KV_WIDTH), BF16),
        jax.ShapeDtypeStruct((rows, GLA_KEY_WIDTH), F32),
        jax.ShapeDtypeStruct((rows, GLA_KEY_WIDTH), F32),
        jax.ShapeDtypeStruct((rows, GLA_VALUE_WIDTH), BF16),
        jax.ShapeDtypeStruct((rows, GLA_VALUE_WIDTH), F32),
        jax.ShapeDtypeStruct((rows, 2 * GLA_KEY_WIDTH), F32),
    )
    return pl.pallas_call(
        functools.partial(_inproj_kernel, n_ctx_tiles),
        grid=(rows // tm,),
        in_specs=x_specs + [
            pl.BlockSpec(mods_l.shape, const),
            pl.BlockSpec((1, D_MODEL), const),
            pl.BlockSpec((1,) + w_in_b.shape[1:], lambda i: (layer, 0, 0), pipeline_mode=pl.Buffered(1)),
            pl.BlockSpec((1, HEAD_DIM), const),
            pl.BlockSpec((1, HEAD_DIM), const),
            pl.BlockSpec(w2.shape, const),
            pl.BlockSpec(b2.shape, const),
            pl.BlockSpec((tm, LANE), row),
            pl.BlockSpec((tm, LANE), row),
        ],
        out_specs=tuple(pl.BlockSpec((tm, s.shape[1]), row) for s in out_shapes),
        out_shape=out_shapes,
        compiler_params=_cparams(("arbitrary",)),
        name="inproj",
    )(*xs, mods_l, norm_g, w_in_b, q_gain, k_gain, w2, b2, cos_t, sin_t)


def _with_ones(v):
    return jnp.concatenate([v, jnp.ones_like(v)], axis=1)


def _attn_kernel(n_ctx_qtiles, n_ctx, kv_tile, n_pairs, q_ref, k_ref, v_ref, o_ref, s0_ref, s1_ref, m_ref, acc_ref):
    tq = q_ref.shape[0]
    q = jnp.concatenate([q_ref[:, g * HEAD_DIM:(g + 1) * HEAD_DIM] for g in range(ATTN_GROUP)], axis=0)
    is_ctx = pl.program_id(1) < n_ctx_qtiles

    def finish(acc):
        o = acc[:, :HEAD_DIM] / acc[:, HEAD_DIM:]
        o_ref[...] = jnp.concatenate([o[g * tq:(g + 1) * tq, :] for g in range(ATTN_GROUP)], axis=1).astype(BF16)

    @pl.when(is_ctx)
    def _():
        s = _dot_nt(q, k_ref[0:n_ctx, :])
        p = jnp.exp2(s - jnp.max(s, axis=1, keepdims=True)).astype(BF16)
        finish(_dot(p, _with_ones(v_ref[0:n_ctx, :])))

    @pl.when(jnp.logical_not(is_ctx))
    def _():
        reps = kv_tile // LANE
        m_ref[...] = jnp.full(m_ref.shape, -jnp.inf, F32)
        acc_ref[...] = jnp.zeros(acc_ref.shape, F32)

        def scores(tile):
            start = pl.multiple_of(tile * kv_tile, LANE)
            return _dot_nt(q, k_ref[pl.ds(start, kv_tile), :])

        def update(s_ref, tile):
            start = pl.multiple_of(tile * kv_tile, LANE)
            s = s_ref[...]
            m_prev = m_ref[...]
            m_new = jnp.maximum(m_prev, jnp.max(s, axis=1, keepdims=True))
            alpha = jnp.exp2(m_prev - m_new)
            p = jnp.exp2(s - jnp.tile(m_new, (1, reps))).astype(BF16)
            acc_ref[...] = jnp.tile(alpha, (1, 2)) * acc_ref[...] + _dot(p, _with_ones(v_ref[pl.ds(start, kv_tile), :]))
            m_ref[...] = m_new

        s0_ref[...] = scores(0)

        def pair(jj, carry):
            s1_ref[...] = scores(2 * jj + 1)
            update(s0_ref, 2 * jj)
            s0_ref[...] = scores(jnp.minimum(2 * jj + 2, 2 * n_pairs - 1))
            update(s1_ref, 2 * jj + 1)
            return carry

        lax.fori_loop(0, n_pairs, pair, 0)
        finish(acc_ref[...])


def _attention(q, k, v, n_ctx, tq, kv_tile):
    rows = q.shape[0]
    gw = ATTN_GROUP * HEAD_DIM
    n_tiles = rows // kv_tile
    score = pltpu.VMEM((ATTN_GROUP * tq, kv_tile), F32)
    return pl.pallas_call(
        functools.partial(_attn_kernel, n_ctx // tq, n_ctx, kv_tile, n_tiles // 2),
        grid=(ATTN_KV_HEADS, rows // tq),
        in_specs=[
            pl.BlockSpec((tq, gw), lambda g, i: (i, g)),
            pl.BlockSpec((rows, HEAD_DIM), lambda g, i: (0, g)),
            pl.BlockSpec((rows, HEAD_DIM), lambda g, i: (0, g)),
        ],
        out_specs=pl.BlockSpec((tq, gw), lambda g, i: (i, g)),
        out_shape=jax.ShapeDtypeStruct((rows, ATTN_WIDTH), BF16),
        scratch_shapes=[score, score, pltpu.VMEM((ATTN_GROUP * tq, LANE), F32),
                        pltpu.VMEM((ATTN_GROUP * tq, 2 * HEAD_DIM), F32)],
        compiler_params=_cparams(("arbitrary", "arbitrary")),
        name="attention",
    )(q, k, v)


def _kv_tile(rows):
    for n in range(1536 // LANE, 0, -1):
        if rows % (n * LANE) == 0 and (rows // (n * LANE)) % 2 == 0:
            return n * LANE
    raise ValueError("unsupported token count for the attention key tiling")


def _gla_direction(reverse, q_ref, k_ref, v_ref, la_ref, o_ref, s_ref):
    n = GLA_CHUNK
    row = lax.broadcasted_iota(jnp.int32, (n, n), 0)
    col = lax.broadcasted_iota(jnp.int32, (n, n), 1)
    tri = jnp.where((col >= row) if reverse else (col <= row), 1.0, 0.0).astype(BF16)
    valid_pair = (col >= row) if reverse else (col <= row)
    rows1 = lax.broadcasted_iota(jnp.int32, (n, 1), 0)
    last = 0 if reverse else n - 1
    n_sub = q_ref.shape[0] // n
    b_all = {}
    for c in range(n_sub):
        pieces = _split3(la_ref[c * n:(c + 1) * n, :])
        b_all[c] = _dot(tri, pieces[0]) + _dot(tri, pieces[1]) + _dot(tri, pieces[2])
    for c, h in [(c, h) for c in (range(n_sub - 1, -1, -1) if reverse else range(n_sub)) for h in range(GLA_HEADS)]:
        rs = slice(c * n, (c + 1) * n)
        ks = slice(h * GLA_DK, (h + 1) * GLA_DK)
        q = q_ref[rs, ks]
        k = k_ref[rs, ks]
        v = v_ref[rs, h * GLA_DV:(h + 1) * GLA_DV]
        b = b_all[c][:, ks]
        b_tot = b[last:last + 1, :]
        a_rows = []
        for blk in range(n // GLA_SUB):
            lo, hi = blk * GLA_SUB, (blk + 1) * GLA_SUB
            ref_row = hi - 1 if reverse else lo
            b_ref = b[ref_row:ref_row + 1, :]
            q_blk = q[lo:hi, :] * jnp.exp(b[lo:hi, :] - b_ref)
            reach = (rows1 >= lo) if reverse else (rows1 < hi)
            k_blk = jnp.where(reach, k * jnp.exp(jnp.where(reach, b_ref - b, 0.0)), 0.0)
            a_rows.append(_dot_nt(q_blk.astype(BF16), k_blk.astype(BF16)))
        a = jnp.where(valid_pair, jnp.concatenate(a_rows, axis=0), 0.0).astype(BF16)
        st = s_ref[h]
        o = _dot(a, v) + _dot_nt((q * jnp.exp(b)).astype(BF16), st.astype(BF16))
        o_ref[rs, h * GLA_DV:(h + 1) * GLA_DV] = o
        k_end = (k * jnp.exp(b_tot - b)).astype(BF16)
        s_ref[h] = jnp.exp(b_tot) * st + _dot_tn(v, k_end)


def _gla_kernel(qf, kf, vf, laf, qb, kb, vb, lab, of_ref, ob_ref, sf_ref, sb_ref):
    @pl.when(pl.program_id(0) == 0)
    def _():
        sf_ref[...] = jnp.zeros(sf_ref.shape, F32)
        sb_ref[...] = jnp.zeros(sb_ref.shape, F32)

    _gla_direction(False, qf, kf, vf, laf, of_ref, sf_ref)
    _gla_direction(True, qb, kb, vb, lab, ob_ref, sb_ref)


def _gla(gq, gk, gv, la, n_ctx):
    rows = gq.shape[0]
    step = next(m * GLA_CHUNK for m in (4, 2, 1) if n_ctx % (m * GLA_CHUNK) == 0 and rows % (m * GLA_CHUNK) == 0)
    n_chunks = rows // step
    n_ctx_chunks = n_ctx // step

    def bchunk(s):
        return jnp.where(s < n_ctx_chunks, n_ctx_chunks - 1 - s, n_chunks - 1 - s + n_ctx_chunks)

    fwd = lambda s: (s, 0)
    bwd = lambda s: (bchunk(s), 0)
    bwd_la = lambda s: (bchunk(s), 1)
    key_blk = (step, GLA_KEY_WIDTH)
    val_blk = (step, GLA_VALUE_WIDTH)
    state = pltpu.VMEM((GLA_HEADS, GLA_DV, GLA_DK), F32)
    return pl.pallas_call(
        _gla_kernel,
        grid=(n_chunks,),
        in_specs=[
            pl.BlockSpec(key_blk, fwd), pl.BlockSpec(key_blk, fwd), pl.BlockSpec(val_blk, fwd),
            pl.BlockSpec(key_blk, fwd),
            pl.BlockSpec(key_blk, bwd), pl.BlockSpec(key_blk, bwd), pl.BlockSpec(val_blk, bwd),
            pl.BlockSpec(key_blk, bwd_la),
        ],
        out_specs=(pl.BlockSpec(val_blk, fwd), pl.BlockSpec(val_blk, bwd)),
        out_shape=(jax.ShapeDtypeStruct((rows, GLA_VALUE_WIDTH), F32),) * 2,
        scratch_shapes=[state, state],
        compiler_params=_cparams(("arbitrary",)),
        name="gla_scan",
    )(gq, gk, gv, la, gq, gk, gv, la)


def _outproj_kernel(n_ctx_tiles, xc_ref, xl_ref, a_ref, of_ref, ob_ref, r_ref, m_ref, gg_ref, w_ref, nf_ref,
                    wrh_ref, wrl_ref, xo_ref, h2_ref, aff_ref, wb_ref):
    is_ctx = pl.program_id(0) < n_ctx_tiles

    @pl.when(pl.program_id(0) == 0)
    def _():
        wb_ref[...] = w_ref[0].astype(BF16)

    gla = of_ref[...] + ob_ref[...]
    r = r_ref[...]
    parts = [a_ref[...]]
    for h in range(GLA_HEADS):
        sl = slice(h * GLA_DV, (h + 1) * GLA_DV)
        parts.append((_rms(gla[:, sl], gg_ref[...]) * _silu(r[:, sl])).astype(BF16))
    y = _dot(jnp.concatenate(parts, axis=1), wb_ref[...])
    x = jnp.where(is_ctx, xc_ref[...], xl_ref[...]) + _mod_rows(m_ref, is_ctx, 2) * y
    xo_ref[...] = x
    h2 = _rms(x, nf_ref[...]) * (1.0 + _mod_rows(m_ref, is_ctx, 4)) + _mod_rows(m_ref, is_ctx, 3)
    h2_ref[...] = h2
    h_hi, h_lo = _split2(h2)
    logits = _dot(h_hi, wrh_ref[...]) + _dot(h_lo, wrh_ref[...]) + _dot(h_hi, wrl_ref[...])
    live = lax.broadcasted_iota(jnp.int32, logits.shape, 1) < N_EXPERTS
    logits = jnp.where(live, logits, -jnp.inf)
    e = jnp.exp(logits - jnp.max(logits, axis=-1, keepdims=True))
    aff_ref[...] = e / jnp.sum(e, axis=-1, keepdims=True)


def _outproj(layer, stream, rows, attn_o, o_f, o_b, r, mods_l, gla_gain, w_out, norm_ffn, wr_hi, wr_lo, tm,
             n_ctx_tiles):
    row = lambda i: (i, 0)
    const = lambda i: (0, 0)
    xs, x_specs = _stream_specs(stream, tm, n_ctx_tiles)
    return pl.pallas_call(
        functools.partial(_outproj_kernel, n_ctx_tiles),
        grid=(rows // tm,),
        in_specs=x_specs + [
            pl.BlockSpec((tm, ATTN_WIDTH), row),
            pl.BlockSpec((tm, GLA_VALUE_WIDTH), row),
            pl.BlockSpec((tm, GLA_VALUE_WIDTH), row),
            pl.BlockSpec((tm, GLA_VALUE_WIDTH), row),
            pl.BlockSpec(mods_l.shape, const),
            pl.BlockSpec((1, GLA_DV), const),
            pl.BlockSpec((1,) + w_out.shape[1:], lambda i: (layer, 0, 0), pipeline_mode=pl.Buffered(1)),
            pl.BlockSpec((1, D_MODEL), const),
            pl.BlockSpec(wr_hi.shape, const),
            pl.BlockSpec(wr_lo.shape, const),
        ],
        out_specs=(
            pl.BlockSpec((tm, D_MODEL), row),
            pl.BlockSpec((tm, D_MODEL), row),
            pl.BlockSpec((tm, LANE), row),
        ),
        out_shape=(
            jax.ShapeDtypeStruct((rows, D_MODEL), F32),
            jax.ShapeDtypeStruct((rows, D_MODEL), F32),
            jax.ShapeDtypeStruct((rows, LANE), F32),
        ),
        scratch_shapes=[pltpu.VMEM(w_out.shape[1:], BF16)],
        compiler_params=_cparams(("arbitrary",)),
        name="outproj",
    )(*xs, attn_o, o_f, o_b, r, mods_l, gla_gain, w_out, norm_ffn, wr_hi, wr_lo)


def _exclusive_rank(mask_b):
    nr = mask_b.shape[0]
    li = lax.broadcasted_iota(jnp.int32, (LANE, LANE), 0)
    lj = lax.broadcasted_iota(jnp.int32, (LANE, LANE), 1)
    within = _dot(mask_b, jnp.where(li < lj, 1.0, 0.0).astype(BF16))
    ri = lax.broadcasted_iota(jnp.int32, (nr, nr), 0)
    rj = lax.broadcasted_iota(jnp.int32, (nr, nr), 1)
    before = jnp.sum(_dot(jnp.where(rj < ri, 1.0, 0.0).astype(BF16), mask_b), axis=1, keepdims=True)
    return within + before, before


SEARCH_STEPS = 127
COMPACT_WINDOW = LANE + 16


def _select_kernel(cap, aff_ref, pos_ref, off_ref):
    a = aff_ref[...]
    ne = a.shape[0]

    def count(mask):
        return jnp.sum(jnp.sum(jnp.where(mask, 1.0, 0.0), axis=2, keepdims=True), axis=1, keepdims=True)

    def search(_, carry):
        t, step = carry
        cand = t + step
        return jnp.where(count(a >= cand) >= cap, cand, t), step * 0.5

    thr, _ = lax.fori_loop(0, SEARCH_STEPS, search, (jnp.zeros((ne, 1, 1), F32), jnp.ones((1, 1, 1), F32)))
    room = cap - count(a > thr)
    for e in range(ne):
        above = a[e] > thr[e]
        tie = a[e] == thr[e]
        tie_rank, _ = _exclusive_rank(jnp.where(tie, 1.0, 0.0).astype(BF16))
        chosen = above | (tie & (tie_rank < room[e]))
        slot, before = _exclusive_rank(jnp.where(chosen, 1.0, 0.0).astype(BF16))
        pos_ref[e] = jnp.where(chosen, slot, -1.0)
        off_ref[e] = jnp.broadcast_to(before, slot.shape)


def _compact_kernel(cap, off_ref, pos_ref, aff_ref, out_ref, acc_ref):
    e = pl.program_id(0)
    nr = pos_ref.shape[1]
    acc_ref[...] = jnp.zeros(acc_ref.shape, F32)
    rel_ids = lax.broadcasted_iota(jnp.int32, (COMPACT_WINDOW, LANE), 0).astype(F32)
    lane_row = lax.broadcasted_iota(jnp.int32, (1, LANE), 1).astype(F32)
    pad_rows = jnp.zeros((LANE - SUBLANE, LANE), BF16)

    def gather_row(r, carry):
        start = pl.multiple_of(jnp.minimum((off_ref[e, r] // SUBLANE) * SUBLANE, cap), SUBLANE)
        slot_ids = rel_ids + start.astype(F32)
        onehot = jnp.where(slot_ids == pos_ref[0, pl.ds(r, 1), :], 1.0, 0.0).astype(BF16)
        a_hi, a_mid, a_lo = _split3(aff_ref[0, pl.ds(r, 1), :])
        r_row = jnp.full((1, LANE), r, jnp.int32).astype(F32)
        vals = jnp.concatenate(
            [lane_row.astype(BF16), r_row.astype(BF16), a_hi, a_mid, a_lo, jnp.zeros((3, LANE), BF16), pad_rows],
            axis=0)
        acc_ref[pl.ds(start, COMPACT_WINDOW), :] += _dot_nt(onehot, vals)
        return carry

    lax.fori_loop(0, nr, gather_row, 0, unroll=8)
    acc = acc_ref[0:cap, :]
    token = acc[:, 0:1] + LANE * acc[:, 1:2]
    gate = acc[:, 2:3] + acc[:, 3:4] + acc[:, 4:5]
    lane = lax.broadcasted_iota(jnp.int32, (cap, LANE), 1)
    out_ref[0] = jnp.where(lane == 0, token, jnp.where(lane == 1, gate, 0.0))


def _route(aff_t, cap):
    ne, nr, _ = aff_t.shape
    assert nr <= 256 and nr % SUBLANE == 0 and cap % SUBLANE == 0
    whole = pl.BlockSpec((ne, nr, LANE), lambda i: (0, 0, 0))
    pos, off = pl.pallas_call(
        functools.partial(_select_kernel, cap),
        grid=(1,),
        in_specs=[whole],
        out_specs=(whole, whole),
        out_shape=(jax.ShapeDtypeStruct(aff_t.shape, F32),) * 2,
        compiler_params=_cparams(("arbitrary",)),
        name="route_select",
    )(aff_t)
    per_expert = pl.BlockSpec((1, nr, LANE), lambda e, off: (e, 0, 0))
    return pl.pallas_call(
        functools.partial(_compact_kernel, cap),
        grid_spec=pltpu.PrefetchScalarGridSpec(
            num_scalar_prefetch=1,
            grid=(ne,),
            in_specs=[per_expert, per_expert],
            out_specs=pl.BlockSpec((1, cap, LANE), lambda e, off: (e, 0, 0)),
            scratch_shapes=[pltpu.VMEM((cap + COMPACT_WINDOW, LANE), F32)],
        ),
        out_shape=jax.ShapeDtypeStruct((ne, cap, LANE), F32),
        compiler_params=_cparams(("arbitrary",)),
        name="route_compact",
    )(off[:, :, 0].astype(jnp.int32), pos, aff_t)


def _route_tokens(aff, lo, n, offset):
    cap = CAPACITY_FACTOR * n // N_EXPERTS
    a = aff[lo:lo + n, :N_EXPERTS].T.reshape(N_EXPERTS, n // LANE, LANE)
    pad = (-a.shape[1]) % SUBLANE
    if pad:
        a = jnp.concatenate([a, jnp.full((N_EXPERTS, pad, LANE), -1.0, F32)], axis=1)
    sel = _route(a, cap)
    return sel[:, :, 0].astype(jnp.int32) + offset, sel[:, :, 1]


def _moe_kernel(n_lat, nf, idx_ref, gate_ref, m_ref, wg_ref, wu_ref, wd_ref, h2_hbm, x_hbm, xo_hbm,
                stage_h, stage_x, xg, acc, sem_h, sem_in, sem_out):
    del x_hbm
    e = pl.program_id(0)
    f = pl.program_id(1)
    ne = pl.num_programs(0)
    groups = stage_h.shape[0]
    ns = groups * SUBLANE
    per_step = ns // nf
    down_cols = 4 * LANE

    def row_copy(t, g, i, to_vmem, hbm, stage, sem):
        if to_vmem:
            return pltpu.make_async_copy(hbm.at[pl.ds(t, 1), :], stage.at[g, pl.ds(i, 1), :], sem)
        return pltpu.make_async_copy(stage.at[g, pl.ds(i, 1), :], hbm.at[pl.ds(t, 1), :], sem)

    def start_all_rows(expert, to_vmem, hbm, stage, sem):
        def group(g, carry):
            first = expert * ns + g * SUBLANE
            for i in range(SUBLANE):
                row_copy(idx_ref[first + i], g, i, to_vmem, hbm, stage, sem).start()
            return carry
        lax.fori_loop(0, groups, group, 0)

    def start_step_rows(expert, to_vmem, hbm, stage, sem):
        first = expert * ns + f * per_step
        for j in range(per_step):
            if per_step % SUBLANE == 0:
                g, i = f * (per_step // SUBLANE) + j // SUBLANE, j % SUBLANE
            else:
                s = f * per_step + j
                g, i = s // SUBLANE, s % SUBLANE
            row_copy(idx_ref[first + j], g, i, to_vmem, hbm, stage, sem).start()

    def wait_rows(stage, sem):
        pltpu.make_async_copy(stage, stage, sem).wait()

    @pl.when(f == 0)
    def _():
        @pl.when(e == 0)
        def _():
            start_all_rows(0, True, h2_hbm, stage_h, sem_h)

        wait_rows(stage_h, sem_h)
        xg[...] = stage_h[...].reshape(ns, D_MODEL).astype(BF16)
        acc[...] = jnp.zeros(acc.shape, F32)

    @pl.when(f == nf - 2)
    def _():
        @pl.when(e > 0)
        def _():
            wait_rows(stage_x, sem_out)

        start_all_rows(e, True, xo_hbm, stage_x, sem_in)

    start_step_rows(jnp.minimum(e + 1, ne - 1), True, h2_hbm, stage_h, sem_h)
    x = xg[...]
    hid = (_silu(_dot(x, wg_ref[0, 0].astype(BF16))) * _dot(x, wu_ref[0, 0].astype(BF16))).astype(BF16)
    for n in range(D_MODEL // down_cols):
        cols = slice(n * down_cols, (n + 1) * down_cols)
        part = _dot(hid, wd_ref[0, 0, :, cols].astype(BF16))
        acc[:, cols] += part

    @pl.when(f == nf - 1)
    def _():
        is_lat = lax.broadcasted_iota(jnp.int32, (ns, 1), 0) < n_lat
        gate = gate_ref[0]
        wait_rows(stage_x, sem_in)
        for n in range(D_MODEL // down_cols):
            cols = slice(n * down_cols, (n + 1) * down_cols)
            mcols = slice(5 * D_MODEL + n * down_cols, 5 * D_MODEL + (n + 1) * down_cols)
            scale = gate * jnp.where(is_lat, m_ref[0:1, mcols], m_ref[1:2, mcols])
            y = (acc[:, cols] * scale).reshape(groups, SUBLANE, down_cols)
            stage_x[:, :, cols] = stage_x[:, :, cols] + y
        start_all_rows(e, False, xo_hbm, stage_x, sem_out)

        @pl.when(e == ne - 1)
        def _():
            wait_rows(stage_x, sem_out)
            wait_rows(stage_h, sem_h)


def _moe(layer, idx, gate, mods_l, w_gate, w_up, w_down, h2, x2, n_lat):
    ne, ns = idx.shape
    ft = 256 if ns % (EXPERT_FF // 256) == 0 else 384
    nf = EXPERT_FF // ft
    assert nf >= 2 and ns % nf == 0 and ns % SUBLANE == 0
    stage = pltpu.VMEM((ns // SUBLANE, SUBLANE, D_MODEL), F32)
    grid_spec = pltpu.PrefetchScalarGridSpec(
        num_scalar_prefetch=1,
        grid=(ne, nf),
        in_specs=[
            pl.BlockSpec((1, ns, 1), lambda e, f, idx: (e, 0, 0)),
            pl.BlockSpec(mods_l.shape, lambda e, f, idx: (0, 0)),
            pl.BlockSpec((1, 1, D_MODEL, ft), lambda e, f, idx: (layer, e, 0, f)),
            pl.BlockSpec((1, 1, D_MODEL, ft), lambda e, f, idx: (layer, e, 0, f)),
            pl.BlockSpec((1, 1, ft, D_MODEL), lambda e, f, idx: (layer, e, f, 0)),
            pl.BlockSpec(memory_space=pl.ANY),
            pl.BlockSpec(memory_space=pl.ANY),
        ],
        out_specs=pl.BlockSpec(memory_space=pl.ANY),
        scratch_shapes=[
            stage,
            stage,
            pltpu.VMEM((ns, D_MODEL), BF16),
            pltpu.VMEM((ns, D_MODEL), F32),
            pltpu.SemaphoreType.DMA,
            pltpu.SemaphoreType.DMA,
            pltpu.SemaphoreType.DMA,
        ],
    )
    return pl.pallas_call(
        functools.partial(_moe_kernel, n_lat, nf),
        grid_spec=grid_spec,
        out_shape=jax.ShapeDtypeStruct(x2.shape, F32),
        input_output_aliases={7: 0},
        compiler_params=_cparams(("arbitrary", "arbitrary")),
        name="moe",
    )(idx.reshape(-1), gate[:, :, None], mods_l, w_gate, w_up, w_down, h2, x2)


def _final_kernel(x_ref, g_ref, o_ref):
    o_ref[...] = _rms(x_ref[...], g_ref[...])


def _final_norm(x2, gain, n_ctx, tm):
    n_lat = x2.shape[0] - n_ctx
    skip = n_ctx // tm
    return pl.pallas_call(
        _final_kernel,
        grid=(n_lat // tm,),
        in_specs=[
            pl.BlockSpec((tm, D_MODEL), lambda i: (i + skip, 0)),
            pl.BlockSpec((1, D_MODEL), lambda i: (0, 0)),
        ],
        out_specs=pl.BlockSpec((tm, D_MODEL), lambda i: (i, 0)),
        out_shape=jax.ShapeDtypeStruct((n_lat, D_MODEL), F32),
        compiler_params=_cparams(("arbitrary",)),
        name="final_norm",
    )(x2, gain)


def _rope_tables(n_lat, n_ctx):
    rows = n_lat // GRID_W
    row = jnp.repeat(jnp.arange(rows, dtype=F32), GRID_W)
    col = jnp.tile(jnp.arange(GRID_W, dtype=F32), rows)
    half = HEAD_DIM // 2
    inv_freq = ROPE_THETA ** (-jnp.arange(0, half, 2, dtype=F32) / half)
    ang = jnp.concatenate([row[:, None] * inv_freq, col[:, None] * inv_freq], axis=-1)
    cos = jnp.repeat(jnp.cos(ang), 2, axis=-1)
    sign = jnp.tile(jnp.array([-1.0, 1.0], F32), half)
    sin = jnp.repeat(jnp.sin(ang), 2, axis=-1) * sign
    cos = jnp.concatenate([jnp.ones((n_ctx, HEAD_DIM), F32), cos], axis=0)
    sin = jnp.concatenate([jnp.zeros((n_ctx, HEAD_DIM), F32), sin], axis=0)
    return cos, sin


def _row_tile(n_lat, n_ctx):
    for tm in (256, 128, 64):
        if n_lat % tm == 0 and n_ctx % tm == 0:
            return tm
    raise ValueError("token counts must be multiples of 64")


def kernel(x, c, ctx, c_ctx, w_mod, b_mod, norm_mix, w_in, q_gain, k_gain, w_gla_a2, b_gla_a, gla_gain,
           w_out, norm_ffn, w_router, w_gate, w_up, w_down, final_norm):
    batch, n_lat, _ = x.shape
    n_ctx = ctx.shape[1]
    depth = w_mod.shape[0]
    assert batch == 1 and c.shape[0] == 1
    tm = _row_tile(n_lat, n_ctx)
    n_ctx_tiles = n_ctx // tm
    tq = next(t for t in (256, 128, 64) if n_ctx % t == 0)
    kv_tile = _kv_tile(n_ctx + n_lat)

    cvec = jnp.concatenate([c, c_ctx[None, :], jnp.zeros((SUBLANE - 2, D_MODEL), F32)], axis=0)
    mods = _modulation(cvec, w_mod, b_mod)
    cos_t, sin_t = _rope_tables(n_lat, n_ctx)
    rows = n_ctx + n_lat
    stream = (ctx[0], x[0], 0)

    w_in_b = _cast_pad_in_weights(w_in)
    for l in range(depth):
        last = l == depth - 1
        w2 = jnp.zeros((LANE, 2 * GLA_KEY_WIDTH), F32)
        o0 = IN_WIDTH - 2 * GLA_GATE_RANK - GATE_COL
        w2 = w2.at[o0:o0 + GLA_GATE_RANK, :GLA_KEY_WIDTH].set(w_gla_a2[l, 0])
        w2 = w2.at[o0 + GLA_GATE_RANK:o0 + 2 * GLA_GATE_RANK, GLA_KEY_WIDTH:].set(w_gla_a2[l, 1])
        b2 = b_gla_a[l].reshape(1, 2 * GLA_KEY_WIDTH)
        q, k, v, gq, gk, gv, r, la = _inproj(
            l, stream, rows, mods[l], norm_mix[l][None, :], w_in_b, q_gain[l][None, :], k_gain[l][None, :], w2, b2,
            cos_t, sin_t, tm, n_ctx_tiles)
        attn_o = _attention(q, k, v, n_ctx, tq, kv_tile)
        o_f, o_b = _gla(gq, gk, gv, la, n_ctx)
        wr = jnp.pad(w_router[l], ((0, 0), (0, LANE - N_EXPERTS)))
        wr_hi, wr_lo = _split2(wr)
        x2, h2, aff = _outproj(l, stream, rows, attn_o, o_f, o_b, r, mods[l], gla_gain[l][None, :],
                               w_out, norm_ffn[l][None, :], wr_hi, wr_lo, tm, n_ctx_tiles)
        idx, gate = _route_tokens(aff, n_ctx, n_lat, n_ctx)
        n_lat_slots = idx.shape[1]
        if not last:
            idx_c, gate_c = _route_tokens(aff, 0, n_ctx, 0)
            idx = jnp.concatenate([idx, idx_c], axis=1)
            gate = jnp.concatenate([gate, gate_c], axis=1)
        x2 = _moe(l, idx, gate, mods[l], w_gate, w_up, w_down, h2, x2, n_lat_slots)
        stream = (x2, x2, n_ctx_tiles)

    return _final_norm(x2, final_norm[None, :], n_ctx, tm)[None]
```

```python
import functools
import math

import jax
import jax.numpy as jnp
from jax import lax
from jax.experimental import pallas as pl
from jax.experimental.pallas import tpu as pltpu

D_MODEL = 2048
HEAD_DIM = 128
ATTN_HEADS = 8
ATTN_KV_HEADS = 2
ATTN_GROUP = ATTN_HEADS // ATTN_KV_HEADS
ATTN_WIDTH = ATTN_HEADS * HEAD_DIM
KV_WIDTH = ATTN_KV_HEADS * HEAD_DIM
GLA_HEADS = 4
GLA_DK = 128
GLA_DV = 256
GLA_KEY_WIDTH = GLA_HEADS * GLA_DK
GLA_VALUE_WIDTH = GLA_HEADS * GLA_DV
GLA_GATE_RANK = 16
GLA_GATE_TEMP = 16.0
GLA_CHUNK = 64
GLA_SUB = 16
N_EXPERTS = 16
EXPERT_FF = 1536
CAPACITY_FACTOR = 2
GRID_W = 64
ROPE_THETA = 10000.0
EPS = 1e-6
IN_WIDTH = ATTN_WIDTH + 2 * KV_WIDTH + 2 * GLA_KEY_WIDTH + 2 * GLA_VALUE_WIDTH + 2 * GLA_GATE_RANK

LANE = 128
SUBLANE = 8
VMEM_LIMIT_BYTES = 56 * 1024 * 1024

IN_WIDTH_PAD = ((IN_WIDTH + LANE - 1) // LANE) * LANE
GATE_COL = IN_WIDTH_PAD - LANE

F32 = jnp.float32
BF16 = jnp.bfloat16


def _cparams(sem):
    return pltpu.CompilerParams(dimension_semantics=sem, vmem_limit_bytes=VMEM_LIMIT_BYTES)


def _dot(a, b):
    return jnp.dot(a, b, preferred_element_type=F32)


def _dot_nt(a, b):
    return lax.dot_general(a, b, (((1,), (1,)), ((), ())), preferred_element_type=F32)


def _dot_tn(a, b):
    return lax.dot_general(a, b, (((0,), (0,)), ((), ())), preferred_element_type=F32)


def _split2(a):
    hi = a.astype(BF16)
    lo = (a - hi.astype(F32)).astype(BF16)
    return hi, lo


def _split3(a):
    hi = a.astype(BF16)
    r1 = a - hi.astype(F32)
    mid = r1.astype(BF16)
    lo = (r1 - mid.astype(F32)).astype(BF16)
    return hi, mid, lo


def _dot_split(a, w):
    a_hi, a_lo = _split2(a)
    w_hi, w_lo = _split2(w)
    return _dot(a_hi, w_hi) + _dot(a_lo, w_hi) + _dot(a_hi, w_lo)


def _sigmoid(x):
    return 1.0 / (1.0 + jnp.exp(-x))


def _silu(x):
    return x * _sigmoid(x)


def _rms(x, gain):
    return x * lax.rsqrt(jnp.mean(x * x, axis=-1, keepdims=True) + EPS) * gain


def _mod_kernel(c_ref, w_ref, b_ref, o_ref):
    o_ref[0] = _dot_split(_silu(c_ref[...]), w_ref[0]) + b_ref[0]


def _modulation(cvec, w_mod, b_mod):
    depth, _, width = w_mod.shape
    tile = 1536
    return pl.pallas_call(
        _mod_kernel,
        grid=(depth, width // tile),
        in_specs=[
            pl.BlockSpec((SUBLANE, D_MODEL), lambda l, j: (0, 0)),
            pl.BlockSpec((1, D_MODEL, tile), lambda l, j: (l, 0, j)),
            pl.BlockSpec((1, 1, tile), lambda l, j: (l, 0, j)),
        ],
        out_specs=pl.BlockSpec((1, SUBLANE, tile), lambda l, j: (l, 0, j)),
        out_shape=jax.ShapeDtypeStruct((depth, SUBLANE, width), F32),
        compiler_params=_cparams(("arbitrary", "arbitrary")),
        name="modulation",
    )(cvec, w_mod, b_mod[:, None, :])


def _mod_rows(m_ref, is_ctx, k):
    lo, hi = k * D_MODEL, (k + 1) * D_MODEL
    return jnp.where(is_ctx, m_ref[1:2, lo:hi], m_ref[0:1, lo:hi])


def _rope(x, cos, sin_signed, even):
    partner = jnp.where(even, pltpu.roll(x, LANE - 1, 1), pltpu.roll(x, 1, 1))
    return x * cos + partner * sin_signed


def _stream_specs(stream, tm, n_ctx_tiles):
    xc, xl, lat_first = stream
    ctx_map = lambda i: (jnp.minimum(i, n_ctx_tiles - 1), 0)
    lat_map = lambda i: (jnp.maximum(i - n_ctx_tiles, 0) + lat_first, 0)
    return (xc, xl), [pl.BlockSpec((tm, D_MODEL), ctx_map), pl.BlockSpec((tm, D_MODEL), lat_map)]


def _inproj_kernel(n_ctx_tiles, xc_ref, xl_ref, m_ref, g_ref, w_ref, qg_ref, kg_ref, w2_ref, b2_ref, cos_ref, sin_ref,
                   q_ref, k_ref, v_ref, gq_ref, gk_ref, gv_ref, r_ref, la_ref):
    is_ctx = pl.program_id(0) < n_ctx_tiles
    x = jnp.where(is_ctx, xc_ref[...], xl_ref[...])
    h = _rms(x, g_ref[...]) * (1.0 + _mod_rows(m_ref, is_ctx, 1)) + _mod_rows(m_ref, is_ctx, 0)
    hb = h.astype(BF16)
    cos = cos_ref[...]
    sin = sin_ref[...]
    even = (lax.broadcasted_iota(jnp.int32, cos.shape, 1) % 2) == 0
    q_scale = HEAD_DIM ** -0.5 * math.log2(math.e)

    def head(p, gain, scale):
        y = _rope(_rms(p, gain), cos, sin, even)
        return (y * scale).astype(BF16) if scale != 1.0 else y.astype(BF16)

    pq = _dot(hb, w_ref[0, :,0:ATTN_WIDTH])
    for hh in range(ATTN_HEADS):
        sl = slice(hh * HEAD_DIM, (hh + 1) * HEAD_DIM)
        q_ref[:, sl] = head(pq[:, sl], qg_ref[...], q_scale)
    c0 = ATTN_WIDTH
    pk = _dot(hb, w_ref[0, :,c0:c0 + KV_WIDTH])
    for hh in range(ATTN_KV_HEADS):
        sl = slice(hh * HEAD_DIM, (hh + 1) * HEAD_DIM)
        k_ref[:, sl] = head(pk[:, sl], kg_ref[...], 1.0)
    c0 += KV_WIDTH
    v_ref[...] = _dot(hb, w_ref[0, :,c0:c0 + KV_WIDTH]).astype(BF16)
    c0 += KV_WIDTH
    gq_ref[...] = _dot(hb, w_ref[0, :,c0:c0 + GLA_KEY_WIDTH]) * (GLA_DK ** -0.5)
    c0 += GLA_KEY_WIDTH
    gk_ref[...] = _dot(hb, w_ref[0, :,c0:c0 + GLA_KEY_WIDTH])
    c0 += GLA_KEY_WIDTH
    gv_ref[...] = _dot(hb, w_ref[0, :,c0:c0 + GLA_VALUE_WIDTH]).astype(BF16)
    c0 += GLA_VALUE_WIDTH
    r_ref[...] = _dot(hb, w_ref[0, :,c0:c0 + GLA_VALUE_WIDTH])
    a = _dot(hb, w_ref[0, :,GATE_COL:GATE_COL + LANE])
    z = _dot_split(a, w2_ref[...]) + b2_ref[...]
    la_ref[...] = (jnp.minimum(z, 0.0) - jnp.log(1.0 + jnp.exp(-jnp.abs(z)))) * (1.0 / GLA_GATE_TEMP)


def _cast_pad_kernel(w_ref, o_ref):
    w = w_ref[0].astype(BF16)
    o_ref[0, :, 0:GATE_COL] = w[:, 0:GATE_COL]
    o_ref[0, :, GATE_COL:] = jnp.concatenate(
        [w[:, GATE_COL:], jnp.zeros((w.shape[0], IN_WIDTH_PAD - IN_WIDTH), BF16)], axis=1)


def _cast_pad_in_weights(w_in):
    depth = w_in.shape[0]
    tk = 256
    return pl.pallas_call(
        _cast_pad_kernel,
        grid=(depth, D_MODEL // tk),
        in_specs=[pl.BlockSpec((1, tk, IN_WIDTH), lambda l, k: (l, k, 0))],
        out_specs=pl.BlockSpec((1, tk, IN_WIDTH_PAD), lambda l, k: (l, k, 0)),
        out_shape=jax.ShapeDtypeStruct((depth, D_MODEL, IN_WIDTH_PAD), BF16),
        compiler_params=_cparams(("arbitrary", "arbitrary")),
        name="cast_pad_w_in",
    )(w_in)


def _inproj(layer, stream, rows, mods_l, norm_g, w_in_b, q_gain, k_gain, w2, b2, cos_t, sin_t, tm, n_ctx_tiles):
    row = lambda i: (i, 0)
    const = lambda i: (0, 0)
    xs, x_specs = _stream_specs(stream, tm, n_ctx_tiles)
    out_shapes = (
        jax.ShapeDtypeStruct((rows, ATTN_WIDTH), BF16),
        jax.ShapeDtypeStruct((rows, KV_WIDTH), BF16),
        jax.ShapeDtypeStruct((rows, KV_WIDTH), BF16),
        jax.ShapeDtypeStruct((rows, GLA_KEY_WIDTH), F32),
        jax.ShapeDtypeStruct((rows, GLA_KEY_WIDTH), F32),
        jax.ShapeDtypeStruct((rows, GLA_VALUE_WIDTH), BF16),
        jax.ShapeDtypeStruct((rows, GLA_VALUE_WIDTH), F32),
        jax.ShapeDtypeStruct((rows, 2 * GLA_KEY_WIDTH), F32),
    )
    return pl.pallas_call(
        functools.partial(_inproj_kernel, n_ctx_tiles),
        grid=(rows // tm,),
        in_specs=x_specs + [
            pl.BlockSpec(mods_l.shape, const),
            pl.BlockSpec((1, D_MODEL), const),
            pl.BlockSpec((1,) + w_in_b.shape[1:], lambda i: (layer, 0, 0), pipeline_mode=pl.Buffered(1)),
            pl.BlockSpec((1, HEAD_DIM), const),
            pl.BlockSpec((1, HEAD_DIM), const),
            pl.BlockSpec(w2.shape, const),
            pl.BlockSpec(b2.shape, const),
            pl.BlockSpec((tm, LANE), row),
            pl.BlockSpec((tm, LANE), row),
        ],
        out_specs=tuple(pl.BlockSpec((tm, s.shape[1]), row) for s in out_shapes),
        out_shape=out_shapes,
        compiler_params=_cparams(("arbitrary",)),
        name="inproj",
    )(*xs, mods_l, norm_g, w_in_b, q_gain, k_gain, w2, b2, cos_t, sin_t)


def _with_ones(v):
    return jnp.concatenate([v, jnp.ones_like(v)], axis=1)


def _attn_kernel(n_ctx_qtiles, n_ctx, kv_tile, n_pairs, q_ref, k_ref, v_ref, o_ref, s0_ref, s1_ref, m_ref, acc_ref):
    tq = q_ref.shape[0]
    q = jnp.concatenate([q_ref[:, g * HEAD_DIM:(g + 1) * HEAD_DIM] for g in range(ATTN_GROUP)], axis=0)
    is_ctx = pl.program_id(1) < n_ctx_qtiles

    def finish(acc):
        o = acc[:, :HEAD_DIM] / acc[:, HEAD_DIM:]
        o_ref[...] = jnp.concatenate([o[g * tq:(g + 1) * tq, :] for g in range(ATTN_GROUP)], axis=1).astype(BF16)

    @pl.when(is_ctx)
    def _():
        s = _dot_nt(q, k_ref[0:n_ctx, :])
        p = jnp.exp2(s - jnp.max(s, axis=1, keepdims=True)).astype(BF16)
        finish(_dot(p, _with_ones(v_ref[0:n_ctx, :])))

    @pl.when(jnp.logical_not(is_ctx))
    def _():
        reps = kv_tile // LANE
        m_ref[...] = jnp.full(m_ref.shape, -jnp.inf, F32)
        acc_ref[...] = jnp.zeros(acc_ref.shape, F32)

        def scores(tile):
            start = pl.multiple_of(tile * kv_tile, LANE)
            return _dot_nt(q, k_ref[pl.ds(start, kv_tile), :])

        def update(s_ref, tile):
            start = pl.multiple_of(tile * kv_tile, LANE)
            s = s_ref[...]
            m_prev = m_ref[...]
            m_new = jnp.maximum(m_prev, jnp.max(s, axis=1, keepdims=True))
            alpha = jnp.exp2(m_prev - m_new)
            p = jnp.exp2(s - jnp.tile(m_new, (1, reps))).astype(BF16)
            acc_ref[...] = jnp.tile(alpha, (1, 2)) * acc_ref[...] + _dot(p, _with_ones(v_ref[pl.ds(start, kv_tile), :]))
            m_ref[...] = m_new

        s0_ref[...] = scores(0)

        def pair(jj, carry):
            s1_ref[...] = scores(2 * jj + 1)
            update(s0_ref, 2 * jj)
            s0_ref[...] = scores(jnp.minimum(2 * jj + 2, 2 * n_pairs - 1))
            update(s1_ref, 2 * jj + 1)
            return carry

        lax.fori_loop(0, n_pairs, pair, 0)
        finish(acc_ref[...])


def _attention(q, k, v, n_ctx, tq, kv_tile):
    rows = q.shape[0]
    gw = ATTN_GROUP * HEAD_DIM
    n_tiles = rows // kv_tile
    score = pltpu.VMEM((ATTN_GROUP * tq, kv_tile), F32)
    return pl.pallas_call(
        functools.partial(_attn_kernel, n_ctx // tq, n_ctx, kv_tile, n_tiles // 2),
        grid=(ATTN_KV_HEADS, rows // tq),
        in_specs=[
            pl.BlockSpec((tq, gw), lambda g, i: (i, g)),
            pl.BlockSpec((rows, HEAD_DIM), lambda g, i: (0, g)),
            pl.BlockSpec((rows, HEAD_DIM), lambda g, i: (0, g)),
        ],
        out_specs=pl.BlockSpec((tq, gw), lambda g, i: (i, g)),
        out_shape=jax.ShapeDtypeStruct((rows, ATTN_WIDTH), BF16),
        scratch_shapes=[score, score, pltpu.VMEM((ATTN_GROUP * tq, LANE), F32),
                        pltpu.VMEM((ATTN_GROUP * tq, 2 * HEAD_DIM), F32)],
        compiler_params=_cparams(("arbitrary", "arbitrary")),
        name="attention",
    )(q, k, v)


def _kv_tile(rows):
    for n in range(1536 // LANE, 0, -1):
        if rows % (n * LANE) == 0 and (rows // (n * LANE)) % 2 == 0:
            return n * LANE
    raise ValueError("unsupported token count for the attention key tiling")


def _gla_direction(reverse, q_ref, k_ref, v_ref, la_ref, o_ref, s_ref):
    n = GLA_CHUNK
    row = lax.broadcasted_iota(jnp.int32, (n, n), 0)
    col = lax.broadcasted_iota(jnp.int32, (n, n), 1)
    tri = jnp.where((col >= row) if reverse else (col <= row), 1.0, 0.0).astype(BF16)
    valid_pair = (col >= row) if reverse else (col <= row)
    rows1 = lax.broadcasted_iota(jnp.int32, (n, 1), 0)
    last = 0 if reverse else n - 1
    n_sub = q_ref.shape[0] // n
    b_all = {}
    for c in range(n_sub):
        pieces = _split3(la_ref[c * n:(c + 1) * n, :])
        b_all[c] = _dot(tri, pieces[0]) + _dot(tri, pieces[1]) + _dot(tri, pieces[2])
    for c, h in [(c, h) for c in (range(n_sub - 1, -1, -1) if reverse else range(n_sub)) for h in range(GLA_HEADS)]:
        rs = slice(c * n, (c + 1) * n)
        ks = slice(h * GLA_DK, (h + 1) * GLA_DK)
        q = q_ref[rs, ks]
        k = k_ref[rs, ks]
        v = v_ref[rs, h * GLA_DV:(h + 1) * GLA_DV]
        b = b_all[c][:, ks]
        b_tot = b[last:last + 1, :]
        a_rows = []
        for blk in range(n // GLA_SUB):
            lo, hi = blk * GLA_SUB, (blk + 1) * GLA_SUB
            ref_row = hi - 1 if reverse else lo
            b_ref = b[ref_row:ref_row + 1, :]
            q_blk = q[lo:hi, :] * jnp.exp(b[lo:hi, :] - b_ref)
            reach = (rows1 >= lo) if reverse else (rows1 < hi)
            k_blk = jnp.where(reach, k * jnp.exp(jnp.where(reach, b_ref - b, 0.0)), 0.0)
            a_rows.append(_dot_nt(q_blk.astype(BF16), k_blk.astype(BF16)))
        a = jnp.where(valid_pair, jnp.concatenate(a_rows, axis=0), 0.0).astype(BF16)
        st = s_ref[h]
        o = _dot(a, v) + _dot_nt((q * jnp.exp(b)).astype(BF16), st.astype(BF16))
        o_ref[rs, h * GLA_DV:(h + 1) * GLA_DV] = o
        k_end = (k * jnp.exp(b_tot - b)).astype(BF16)
        s_ref[h] = jnp.exp(b_tot) * st + _dot_tn(v, k_end)


def _gla_kernel(qf, kf, vf, laf, qb, kb, vb, lab, of_ref, ob_ref, sf_ref, sb_ref):
    @pl.when(pl.program_id(0) == 0)
    def _():
        sf_ref[...] = jnp.zeros(sf_ref.shape, F32)
        sb_ref[...] = jnp.zeros(sb_ref.shape, F32)

    _gla_direction(False, qf, kf, vf, laf, of_ref, sf_ref)
    _gla_direction(True, qb, kb, vb, lab, ob_ref, sb_ref)


def _gla(gq, gk, gv, la, n_ctx):
    rows = gq.shape[0]
    step = next(m * GLA_CHUNK for m in (4, 2, 1) if n_ctx % (m * GLA_CHUNK) == 0 and rows % (m * GLA_CHUNK) == 0)
    n_chunks = rows // step
    n_ctx_chunks = n_ctx // step

    def bchunk(s):
        return jnp.where(s < n_ctx_chunks, n_ctx_chunks - 1 - s, n_chunks - 1 - s + n_ctx_chunks)

    fwd = lambda s: (s, 0)
    bwd = lambda s: (bchunk(s), 0)
    bwd_la = lambda s: (bchunk(s), 1)
    key_blk = (step, GLA_KEY_WIDTH)
    val_blk = (step, GLA_VALUE_WIDTH)
    state = pltpu.VMEM((GLA_HEADS, GLA_DV, GLA_DK), F32)
    return pl.pallas_call(
        _gla_kernel,
        grid=(n_chunks,),
        in_specs=[
            pl.BlockSpec(key_blk, fwd), pl.BlockSpec(key_blk, fwd), pl.BlockSpec(val_blk, fwd),
            pl.BlockSpec(key_blk, fwd),
            pl.BlockSpec(key_blk, bwd), pl.BlockSpec(key_blk, bwd), pl.BlockSpec(val_blk, bwd),
            pl.BlockSpec(key_blk, bwd_la),
        ],
        out_specs=(pl.BlockSpec(val_blk, fwd), pl.BlockSpec(val_blk, bwd)),
        out_shape=(jax.ShapeDtypeStruct((rows, GLA_VALUE_WIDTH), F32),) * 2,
        scratch_shapes=[state, state],
        compiler_params=_cparams(("arbitrary",)),
        name="gla_scan",
    )(gq, gk, gv, la, gq, gk, gv, la)


def _outproj_kernel(n_ctx_tiles, xc_ref, xl_ref, a_ref, of_ref, ob_ref, r_ref, m_ref, gg_ref, w_ref, nf_ref,
                    wrh_ref, wrl_ref, xo_ref, h2_ref, aff_ref, wb_ref):
    is_ctx = pl.program_id(0) < n_ctx_tiles

    @pl.when(pl.program_id(0) == 0)
    def _():
        wb_ref[...] = w_ref[0].astype(BF16)

    gla = of_ref[...] + ob_ref[...]
    r = r_ref[...]
    parts = [a_ref[...]]
    for h in range(GLA_HEADS):
        sl = slice(h * GLA_DV, (h + 1) * GLA_DV)
        parts.append((_rms(gla[:, sl], gg_ref[...]) * _silu(r[:, sl])).astype(BF16))
    y = _dot(jnp.concatenate(parts, axis=1), wb_ref[...])
    x = jnp.where(is_ctx, xc_ref[...], xl_ref[...]) + _mod_rows(m_ref, is_ctx, 2) * y
    xo_ref[...] = x
    h2 = _rms(x, nf_ref[...]) * (1.0 + _mod_rows(m_ref, is_ctx, 4)) + _mod_rows(m_ref, is_ctx, 3)
    h2_ref[...] = h2
    h_hi, h_lo = _split2(h2)
    logits = _dot(h_hi, wrh_ref[...]) + _dot(h_lo, wrh_ref[...]) + _dot(h_hi, wrl_ref[...])
    live = lax.broadcasted_iota(jnp.int32, logits.shape, 1) < N_EXPERTS
    logits = jnp.where(live, logits, -jnp.inf)
    e = jnp.exp(logits - jnp.max(logits, axis=-1, keepdims=True))
    aff_ref[...] = e / jnp.sum(e, axis=-1, keepdims=True)


def _outproj(layer, stream, rows, attn_o, o_f, o_b, r, mods_l, gla_gain, w_out, norm_ffn, wr_hi, wr_lo, tm,
             n_ctx_tiles):
    row = lambda i: (i, 0)
    const = lambda i: (0, 0)
    xs, x_specs = _stream_specs(stream, tm, n_ctx_tiles)
    return pl.pallas_call(
        functools.partial(_outproj_kernel, n_ctx_tiles),
        grid=(rows // tm,),
        in_specs=x_specs + [
            pl.BlockSpec((tm, ATTN_WIDTH), row),
            pl.BlockSpec((tm, GLA_VALUE_WIDTH), row),
            pl.BlockSpec((tm, GLA_VALUE_WIDTH), row),
            pl.BlockSpec((tm, GLA_VALUE_WIDTH), row),
            pl.BlockSpec(mods_l.shape, const),
            pl.BlockSpec((1, GLA_DV), const),
            pl.BlockSpec((1,) + w_out.shape[1:], lambda i: (layer, 0, 0), pipeline_mode=pl.Buffered(1)),
            pl.BlockSpec((1, D_MODEL), const),
            pl.BlockSpec(wr_hi.shape, const),
            pl.BlockSpec(wr_lo.shape, const),
        ],
        out_specs=(
            pl.BlockSpec((tm, D_MODEL), row),
            pl.BlockSpec((tm, D_MODEL), row),
            pl.BlockSpec((tm, LANE), row),
        ),
        out_shape=(
            jax.ShapeDtypeStruct((rows, D_MODEL), F32),
            jax.ShapeDtypeStruct((rows, D_MODEL), F32),
            jax.ShapeDtypeStruct((rows, LANE), F32),
        ),
        scratch_shapes=[pltpu.VMEM(w_out.shape[1:], BF16)],
        compiler_params=_cparams(("arbitrary",)),
        name="outproj",
    )(*xs, attn_o, o_f, o_b, r, mods_l, gla_gain, w_out, norm_ffn, wr_hi, wr_lo)


def _exclusive_rank(mask_b):
    nr = mask_b.shape[0]
    li = lax.broadcasted_iota(jnp.int32, (LANE, LANE), 0)
    lj = lax.broadcasted_iota(jnp.int32, (LANE, LANE), 1)
    within = _dot(mask_b, jnp.where(li < lj, 1.0, 0.0).astype(BF16))
    ri = lax.broadcasted_iota(jnp.int32, (nr, nr), 0)
    rj = lax.broadcasted_iota(jnp.int32, (nr, nr), 1)
    before = jnp.sum(_dot(jnp.where(rj < ri, 1.0, 0.0).astype(BF16), mask_b), axis=1, keepdims=True)
    return within + before, before


SEARCH_STEPS = 127
COMPACT_WINDOW = LANE + 16


def _select_kernel(cap, aff_ref, pos_ref, off_ref):
    a = aff_ref[...]
    ne = a.shape[0]

    def count(mask):
        return jnp.sum(jnp.sum(jnp.where(mask, 1.0, 0.0), axis=2, keepdims=True), axis=1, keepdims=True)

    def search(_, carry):
        t, step = carry
        cand = t + step
        return jnp.where(count(a >= cand) >= cap, cand, t), step * 0.5

    thr, _ = lax.fori_loop(0, SEARCH_STEPS, search, (jnp.zeros((ne, 1, 1), F32), jnp.ones((1, 1, 1), F32)))
    room = cap - count(a > thr)
    for e in range(ne):
        above = a[e] > thr[e]
        tie = a[e] == thr[e]
        tie_rank, _ = _exclusive_rank(jnp.where(tie, 1.0, 0.0).astype(BF16))
        chosen = above | (tie & (tie_rank < room[e]))
        slot, before = _exclusive_rank(jnp.where(chosen, 1.0, 0.0).astype(BF16))
        pos_ref[e] = jnp.where(chosen, slot, -1.0)
        off_ref[e] = jnp.broadcast_to(before, slot.shape)


def _compact_kernel(cap, off_ref, pos_ref, aff_ref, out_ref, acc_ref):
    e = pl.program_id(0)
    nr = pos_ref.shape[1]
    acc_ref[...] = jnp.zeros(acc_ref.shape, F32)
    rel_ids = lax.broadcasted_iota(jnp.int32, (COMPACT_WINDOW, LANE), 0).astype(F32)
    lane_row = lax.broadcasted_iota(jnp.int32, (1, LANE), 1).astype(F32)
    pad_rows = jnp.zeros((LANE - SUBLANE, LANE), BF16)

    def gather_row(r, carry):
        start = pl.multiple_of(jnp.minimum((off_ref[e, r] // SUBLANE) * SUBLANE, cap), SUBLANE)
        slot_ids = rel_ids + start.astype(F32)
        onehot = jnp.where(slot_ids == pos_ref[0, pl.ds(r, 1), :], 1.0, 0.0).astype(BF16)
        a_hi, a_mid, a_lo = _split3(aff_ref[0, pl.ds(r, 1), :])
        r_row = jnp.full((1, LANE), r, jnp.int32).astype(F32)
        vals = jnp.concatenate(
            [lane_row.astype(BF16), r_row.astype(BF16), a_hi, a_mid, a_lo, jnp.zeros((3, LANE), BF16), pad_rows],
            axis=0)
        acc_ref[pl.ds(start, COMPACT_WINDOW), :] += _dot_nt(onehot, vals)
        return carry

    lax.fori_loop(0, nr, gather_row, 0, unroll=8)
    acc = acc_ref[0:cap, :]
    token = acc[:, 0:1] + LANE * acc[:, 1:2]
    gate = acc[:, 2:3] + acc[:, 3:4] + acc[:, 4:5]
    lane = lax.broadcasted_iota(jnp.int32, (cap, LANE), 1)
    out_ref[0] = jnp.where(lane == 0, token, jnp.where(lane == 1, gate, 0.0))


def _route(aff_t, cap):
    ne, nr, _ = aff_t.shape
    assert nr <= 256 and nr % SUBLANE == 0 and cap % SUBLANE == 0
    whole = pl.BlockSpec((ne, nr, LANE), lambda i: (0, 0, 0))
    pos, off = pl.pallas_call(
        functools.partial(_select_kernel, cap),
        grid=(1,),
        in_specs=[whole],
        out_specs=(whole, whole),
        out_shape=(jax.ShapeDtypeStruct(aff_t.shape, F32),) * 2,
        compiler_params=_cparams(("arbitrary",)),
        name="route_select",
    )(aff_t)
    per_expert = pl.BlockSpec((1, nr, LANE), lambda e, off: (e, 0, 0))
    return pl.pallas_call(
        functools.partial(_compact_kernel, cap),
        grid_spec=pltpu.PrefetchScalarGridSpec(
            num_scalar_prefetch=1,
            grid=(ne,),
            in_specs=[per_expert, per_expert],
            out_specs=pl.BlockSpec((1, cap, LANE), lambda e, off: (e, 0, 0)),
            scratch_shapes=[pltpu.VMEM((cap + COMPACT_WINDOW, LANE), F32)],
        ),
        out_shape=jax.ShapeDtypeStruct((ne, cap, LANE), F32),
        compiler_params=_cparams(("arbitrary",)),
        name="route_compact",
    )(off[:, :, 0].astype(jnp.int32), pos, aff_t)


def _route_tokens(aff, lo, n, offset):
    cap = CAPACITY_FACTOR * n // N_EXPERTS
    a = aff[lo:lo + n, :N_EXPERTS].T.reshape(N_EXPERTS, n // LANE, LANE)
    pad = (-a.shape[1]) % SUBLANE
    if pad:
        a = jnp.concatenate([a, jnp.full((N_EXPERTS, pad, LANE), -1.0, F32)], axis=1)
    sel = _route(a, cap)
    return sel[:, :, 0].astype(jnp.int32) + offset, sel[:, :, 1]


def _moe_kernel(n_lat, nf, idx_ref, gate_ref, m_ref, wg_ref, wu_ref, wd_ref, h2_hbm, x_hbm, xo_hbm,
                stage_h, stage_x, xg, acc, sem_h, sem_in, sem_out):
    del x_hbm
    e = pl.program_id(0)
    f = pl.program_id(1)
    ne = pl.num_programs(0)
    groups = stage_h.shape[0]
    ns = groups * SUBLANE
    per_step = ns // nf
    down_cols = 4 * LANE

    def row_copy(t, g, i, to_vmem, hbm, stage, sem):
        if to_vmem:
            return pltpu.make_async_copy(hbm.at[pl.ds(t, 1), :], stage.at[g, pl.ds(i, 1), :], sem)
        return pltpu.make_async_copy(stage.at[g, pl.ds(i, 1), :], hbm.at[pl.ds(t, 1), :], sem)

    def start_all_rows(expert, to_vmem, hbm, stage, sem):
        def group(g, carry):
            first = expert * ns + g * SUBLANE
            for i in range(SUBLANE):
                row_copy(idx_ref[first + i], g, i, to_vmem, hbm, stage, sem).start()
            return carry
        lax.fori_loop(0, groups, group, 0)

    def start_step_rows(expert, to_vmem, hbm, stage, sem):
        first = expert * ns + f * per_step
        for j in range(per_step):
            if per_step % SUBLANE == 0:
                g, i = f * (per_step // SUBLANE) + j // SUBLANE, j % SUBLANE
            else:
                s = f * per_step + j
                g, i = s // SUBLANE, s % SUBLANE
            row_copy(idx_ref[first + j], g, i, to_vmem, hbm, stage, sem).start()

    def wait_rows(stage, sem):
        pltpu.make_async_copy(stage, stage, sem).wait()

    @pl.when(f == 0)
    def _():
        @pl.when(e == 0)
        def _():
            start_all_rows(0, True, h2_hbm, stage_h, sem_h)

        wait_rows(stage_h, sem_h)
        xg[...] = stage_h[...].reshape(ns, D_MODEL).astype(BF16)
        acc[...] = jnp.zeros(acc.shape, F32)

    @pl.when(f == (nf - 3 if nf > 4 else nf - 2))
    def _():
        @pl.when(e > 0)
        def _():
            wait_rows(stage_x, sem_out)

        start_all_rows(e, True, xo_hbm, stage_x, sem_in)

    start_step_rows(jnp.minimum(e + 1, ne - 1), True, h2_hbm, stage_h, sem_h)
    x = xg[...]
    hid = (_silu(_dot(x, wg_ref[0, 0].astype(BF16))) * _dot(x, wu_ref[0, 0].astype(BF16))).astype(BF16)
    for n in range(D_MODEL // down_cols):
        cols = slice(n * down_cols, (n + 1) * down_cols)
        part = _dot(hid, wd_ref[0, 0, :, cols].astype(BF16))
        acc[:, cols] += part

    @pl.when(f == nf - 1)
    def _():
        is_lat = lax.broadcasted_iota(jnp.int32, (ns, 1), 0) < n_lat
        gate = gate_ref[0]
        wait_rows(stage_x, sem_in)
        for n in range(D_MODEL // down_cols):
            cols = slice(n * down_cols, (n + 1) * down_cols)
            mcols = slice(5 * D_MODEL + n * down_cols, 5 * D_MODEL + (n + 1) * down_cols)
            scale = gate * jnp.where(is_lat, m_ref[0:1, mcols], m_ref[1:2, mcols])
            y = (acc[:, cols] * scale).reshape(groups, SUBLANE, down_cols)
            stage_x[:, :, cols] = stage_x[:, :, cols] + y
        start_all_rows(e, False, xo_hbm, stage_x, sem_out)

        @pl.when(e == ne - 1)
        def _():
            wait_rows(stage_x, sem_out)
            wait_rows(stage_h, sem_h)


def _moe(layer, idx, gate, mods_l, w_gate, w_up, w_down, h2, x2, n_lat):
    ne, ns = idx.shape
    ft = 256 if ns % (EXPERT_FF // 256) == 0 else 384
    nf = EXPERT_FF // ft
    assert nf >= 2 and ns % nf == 0 and ns % SUBLANE == 0
    stage = pltpu.VMEM((ns // SUBLANE, SUBLANE, D_MODEL), F32)
    grid_spec = pltpu.PrefetchScalarGridSpec(
        num_scalar_prefetch=1,
        grid=(ne, nf),
        in_specs=[
            pl.BlockSpec((1, ns, 1), lambda e, f, idx: (e, 0, 0)),
            pl.BlockSpec(mods_l.shape, lambda e, f, idx: (0, 0)),
            pl.BlockSpec((1, 1, D_MODEL, ft), lambda e, f, idx: (layer, e, 0, f)),
            pl.BlockSpec((1, 1, D_MODEL, ft), lambda e, f, idx: (layer, e, 0, f)),
            pl.BlockSpec((1, 1, ft, D_MODEL), lambda e, f, idx: (layer, e, f, 0)),
            pl.BlockSpec(memory_space=pl.ANY),
            pl.BlockSpec(memory_space=pl.ANY),
        ],
        out_specs=pl.BlockSpec(memory_space=pl.ANY),
        scratch_shapes=[
            stage,
            stage,
            pltpu.VMEM((ns, D_MODEL), BF16),
            pltpu.VMEM((ns, D_MODEL), F32),
            pltpu.SemaphoreType.DMA,
            pltpu.SemaphoreType.DMA,
            pltpu.SemaphoreType.DMA,
        ],
    )
    return pl.pallas_call(
        functools.partial(_moe_kernel, n_lat, nf),
        grid_spec=grid_spec,
        out_shape=jax.ShapeDtypeStruct(x2.shape, F32),
        input_output_aliases={7: 0},
        compiler_params=_cparams(("arbitrary", "arbitrary")),
        name="moe",
    )(idx.reshape(-1), gate[:, :, None], mods_l, w_gate, w_up, w_down, h2, x2)


def _final_kernel(x_ref, g_ref, o_ref):
    o_ref[...] = _rms(x_ref[...], g_ref[...])


def _final_norm(x2, gain, n_ctx, tm):
    n_lat = x2.shape[0] - n_ctx
    skip = n_ctx // tm
    return pl.pallas_call(
        _final_kernel,
        grid=(n_lat // tm,),
        in_specs=[
            pl.BlockSpec((tm, D_MODEL), lambda i: (i + skip, 0)),
            pl.BlockSpec((1, D_MODEL), lambda i: (0, 0)),
        ],
        out_specs=pl.BlockSpec((tm, D_MODEL), lambda i: (i, 0)),
        out_shape=jax.ShapeDtypeStruct((n_lat, D_MODEL), F32),
        compiler_params=_cparams(("arbitrary",)),
        name="final_norm",
    )(x2, gain)


def _rope_tables(n_lat, n_ctx):
    rows = n_lat // GRID_W
    row = jnp.repeat(jnp.arange(rows, dtype=F32), GRID_W)
    col = jnp.tile(jnp.arange(GRID_W, dtype=F32), rows)
    half = HEAD_DIM // 2
    inv_freq = ROPE_THETA ** (-jnp.arange(0, half, 2, dtype=F32) / half)
    ang = jnp.concatenate([row[:, None] * inv_freq, col[:, None] * inv_freq], axis=-1)
    cos = jnp.repeat(jnp.cos(ang), 2, axis=-1)
    sign = jnp.tile(jnp.array([-1.0, 1.0], F32), half)
    sin = jnp.repeat(jnp.sin(ang), 2, axis=-1) * sign
    cos = jnp.concatenate([jnp.ones((n_ctx, HEAD_DIM), F32), cos], axis=0)
    sin = jnp.concatenate([jnp.zeros((n_ctx, HEAD_DIM), F32), sin], axis=0)
    return cos, sin


def _row_tile(n_lat, n_ctx):
    for tm in (256, 128, 64):
        if n_lat % tm == 0 and n_ctx % tm == 0:
            return tm
    raise ValueError("token counts must be multiples of 64")


def kernel(x, c, ctx, c_ctx, w_mod, b_mod, norm_mix, w_in, q_gain, k_gain, w_gla_a2, b_gla_a, gla_gain,
           w_out, norm_ffn, w_router, w_gate, w_up, w_down, final_norm):
    batch, n_lat, _ = x.shape
    n_ctx = ctx.shape[1]
    depth = w_mod.shape[0]
    assert batch == 1 and c.shape[0] == 1
    tm = _row_tile(n_lat, n_ctx)
    n_ctx_tiles = n_ctx // tm
    tq = next(t for t in (256, 128, 64) if n_ctx % t == 0)
    kv_tile = _kv_tile(n_ctx + n_lat)

    cvec = jnp.concatenate([c, c_ctx[None, :], jnp.zeros((SUBLANE - 2, D_MODEL), F32)], axis=0)
    mods = _modulation(cvec, w_mod, b_mod)
    cos_t, sin_t = _rope_tables(n_lat, n_ctx)
    rows = n_ctx + n_lat
    stream = (ctx[0], x[0], 0)

    w_in_b = _cast_pad_in_weights(w_in)
    for l in range(depth):
        last = l == depth - 1
        w2 = jnp.zeros((LANE, 2 * GLA_KEY_WIDTH), F32)
        o0 = IN_WIDTH - 2 * GLA_GATE_RANK - GATE_COL
        w2 = w2.at[o0:o0 + GLA_GATE_RANK, :GLA_KEY_WIDTH].set(w_gla_a2[l, 0])
        w2 = w2.at[o0 + GLA_GATE_RANK:o0 + 2 * GLA_GATE_RANK, GLA_KEY_WIDTH:].set(w_gla_a2[l, 1])
        b2 = b_gla_a[l].reshape(1, 2 * GLA_KEY_WIDTH)
        q, k, v, gq, gk, gv, r, la = _inproj(
            l, stream, rows, mods[l], norm_mix[l][None, :], w_in_b, q_gain[l][None, :], k_gain[l][None, :], w2, b2,
            cos_t, sin_t, tm, n_ctx_tiles)
        attn_o = _attention(q, k, v, n_ctx, tq, kv_tile)
        o_f, o_b = _gla(gq, gk, gv, la, n_ctx)
        wr = jnp.pad(w_router[l], ((0, 0), (0, LANE - N_EXPERTS)))
        wr_hi, wr_lo = _split2(wr)
        x2, h2, aff = _outproj(l, stream, rows, attn_o, o_f, o_b, r, mods[l], gla_gain[l][None, :],
                               w_out, norm_ffn[l][None, :], wr_hi, wr_lo, tm, n_ctx_tiles)
        idx, gate = _route_tokens(aff, n_ctx, n_lat, n_ctx)
        n_lat_slots = idx.shape[1]
        if not last:
            idx_c, gate_c = _route_tokens(aff, 0, n_ctx, 0)
            idx = jnp.concatenate([idx, idx_c], axis=1)
            gate = jnp.concatenate([gate, gate_c], axis=1)
        x2 = _moe(l, idx, gate, mods[l], w_gate, w_up, w_down, h2, x2, n_lat_slots)
        stream = (x2, x2, n_ctx_tiles)

    return _final_norm(x2, final_norm[None, :], n_ctx, tm)[None]
```

```python
import functools
import math

import jax
import jax.numpy as jnp
from jax import lax
from jax.experimental import pallas as pl
from jax.experimental.pallas import tpu as pltpu

D_MODEL = 2048
HEAD_DIM = 128
ATTN_HEADS = 8
ATTN_KV_HEADS = 2
ATTN_GROUP = ATTN_HEADS // ATTN_KV_HEADS
ATTN_WIDTH = ATTN_HEADS * HEAD_DIM
KV_WIDTH = ATTN_KV_HEADS * HEAD_DIM
GLA_HEADS = 4
GLA_DK = 128
GLA_DV = 256
GLA_KEY_WIDTH = GLA_HEADS * GLA_DK
GLA_VALUE_WIDTH = GLA_HEADS * GLA_DV
GLA_GATE_RANK = 16
GLA_GATE_TEMP = 16.0
GLA_CHUNK = 64
GLA_SUB = 16
N_EXPERTS = 16
EXPERT_FF = 1536
CAPACITY_FACTOR = 2
GRID_W = 64
ROPE_THETA = 10000.0
EPS = 1e-6
IN_WIDTH = ATTN_WIDTH + 2 * KV_WIDTH + 2 * GLA_KEY_WIDTH + 2 * GLA_VALUE_WIDTH + 2 * GLA_GATE_RANK

LANE = 128
SUBLANE = 8
VMEM_LIMIT_BYTES = 56 * 1024 * 1024

IN_WIDTH_PAD = ((IN_WIDTH + LANE - 1) // LANE) * LANE
GATE_COL = IN_WIDTH_PAD - LANE

F32 = jnp.float32
BF16 = jnp.bfloat16


def _cparams(sem):
    return pltpu.CompilerParams(dimension_semantics=sem, vmem_limit_bytes=VMEM_LIMIT_BYTES)


def _dot(a, b):
    return jnp.dot(a, b, preferred_element_type=F32)


def _dot_nt(a, b):
    return lax.dot_general(a, b, (((1,), (1,)), ((), ())), preferred_element_type=F32)


def _dot_tn(a, b):
    return lax.dot_general(a, b, (((0,), (0,)), ((), ())), preferred_element_type=F32)


def _split2(a):
    hi = a.astype(BF16)
    lo = (a - hi.astype(F32)).astype(BF16)
    return hi, lo


def _split3(a):
    hi = a.astype(BF16)
    r1 = a - hi.astype(F32)
    mid = r1.astype(BF16)
    lo = (r1 - mid.astype(F32)).astype(BF16)
    return hi, mid, lo


def _dot_split(a, w):
    a_hi, a_lo = _split2(a)
    w_hi, w_lo = _split2(w)
    return _dot(a_hi, w_hi) + _dot(a_lo, w_hi) + _dot(a_hi, w_lo)


def _sigmoid(x):
    return 1.0 / (1.0 + jnp.exp(-x))


def _silu(x):
    return x * _sigmoid(x)


def _rms(x, gain):
    return x * lax.rsqrt(jnp.mean(x * x, axis=-1, keepdims=True) + EPS) * gain


def _mod_kernel(c_ref, w_ref, b_ref, o_ref):
    o_ref[0] = _dot_split(_silu(c_ref[...]), w_ref[0]) + b_ref[0]


def _modulation(cvec, w_mod, b_mod):
    depth, _, width = w_mod.shape
    tile = 1536
    return pl.pallas_call(
        _mod_kernel,
        grid=(depth, width // tile),
        in_specs=[
            pl.BlockSpec((SUBLANE, D_MODEL), lambda l, j: (0, 0)),
            pl.BlockSpec((1, D_MODEL, tile), lambda l, j: (l, 0, j)),
            pl.BlockSpec((1, 1, tile), lambda l, j: (l, 0, j)),
        ],
        out_specs=pl.BlockSpec((1, SUBLANE, tile), lambda l, j: (l, 0, j)),
        out_shape=jax.ShapeDtypeStruct((depth, SUBLANE, width), F32),
        compiler_params=_cparams(("arbitrary", "arbitrary")),
        name="modulation",
    )(cvec, w_mod, b_mod[:, None, :])


def _mod_rows(m_ref, is_ctx, k):
    lo, hi = k * D_MODEL, (k + 1) * D_MODEL
    return jnp.where(is_ctx, m_ref[1:2, lo:hi], m_ref[0:1, lo:hi])


def _rope(x, cos, sin_signed, even):
    partner = jnp.where(even, pltpu.roll(x, LANE - 1, 1), pltpu.roll(x, 1, 1))
    return x * cos + partner * sin_signed


def _stream_specs(stream, tm, n_ctx_tiles):
    xc, xl, lat_first = stream
    ctx_map = lambda i: (jnp.minimum(i, n_ctx_tiles - 1), 0)
    lat_map = lambda i: (jnp.maximum(i - n_ctx_tiles, 0) + lat_first, 0)
    return (xc, xl), [pl.BlockSpec((tm, D_MODEL), ctx_map), pl.BlockSpec((tm, D_MODEL), lat_map)]


def _inproj_kernel(n_ctx_tiles, xc_ref, xl_ref, m_ref, g_ref, w_ref, qg_ref, kg_ref, w2_ref, b2_ref, cos_ref, sin_ref,
                   q_ref, k_ref, v_ref, gq_ref, gk_ref, gv_ref, r_ref, la_ref):
    is_ctx = pl.program_id(0) < n_ctx_tiles
    x = jnp.where(is_ctx, xc_ref[...], xl_ref[...])
    h = _rms(x, g_ref[...]) * (1.0 + _mod_rows(m_ref, is_ctx, 1)) + _mod_rows(m_ref, is_ctx, 0)
    hb = h.astype(BF16)
    cos = cos_ref[...]
    sin = sin_ref[...]
    even = (lax.broadcasted_iota(jnp.int32, cos.shape, 1) % 2) == 0
    q_scale = HEAD_DIM ** -0.5 * math.log2(math.e)

    def head(p, gain, scale):
        y = _rope(_rms(p, gain), cos, sin, even)
        return (y * scale).astype(BF16) if scale != 1.0 else y.astype(BF16)

    pq = _dot(hb, w_ref[0, :,0:ATTN_WIDTH])
    for hh in range(ATTN_HEADS):
        sl = slice(hh * HEAD_DIM, (hh + 1) * HEAD_DIM)
        q_ref[:, sl] = head(pq[:, sl], qg_ref[...], q_scale)
    c0 = ATTN_WIDTH
    pk = _dot(hb, w_ref[0, :,c0:c0 + KV_WIDTH])
    for hh in range(ATTN_KV_HEADS):
        sl = slice(hh * HEAD_DIM, (hh + 1) * HEAD_DIM)
        k_ref[:, sl] = head(pk[:, sl], kg_ref[...], 1.0)
    c0 += KV_WIDTH
    v_ref[...] = _dot(hb, w_ref[0, :,c0:c0 + KV_WIDTH]).astype(BF16)
    c0 += KV_WIDTH
    gq_ref[...] = _dot(hb, w_ref[0, :,c0:c0 + GLA_KEY_WIDTH]) * (GLA_DK ** -0.5)
    c0 += GLA_KEY_WIDTH
    gk_ref[...] = _dot(hb, w_ref[0, :,c0:c0 + GLA_KEY_WIDTH])
    c0 += GLA_KEY_WIDTH
    gv_ref[...] = _dot(hb, w_ref[0, :,c0:c0 + GLA_VALUE_WIDTH]).astype(BF16)
    c0 += GLA_VALUE_WIDTH
    r_ref[...] = _dot(hb, w_ref[0, :,c0:c0 + GLA_VALUE_WIDTH])
    a = _dot(hb, w_ref[0, :,GATE_COL:GATE_COL + LANE])
    z = _dot_split(a, w2_ref[...]) + b2_ref[...]
    la_ref[...] = (jnp.minimum(z, 0.0) - jnp.log(1.0 + jnp.exp(-jnp.abs(z)))) * (1.0 / GLA_GATE_TEMP)


def _cast_pad_kernel(w_ref, o_ref):
    w = w_ref[0].astype(BF16)
    o_ref[0, :, 0:GATE_COL] = w[:, 0:GATE_COL]
    o_ref[0, :, GATE_COL:] = jnp.concatenate(
        [w[:, GATE_COL:], jnp.zeros((w.shape[0], IN_WIDTH_PAD - IN_WIDTH), BF16)], axis=1)


def _cast_pad_in_weights(w_in):
    depth = w_in.shape[0]
    tk = 256
    return pl.pallas_call(
        _cast_pad_kernel,
        grid=(depth, D_MODEL // tk),
        in_specs=[pl.BlockSpec((1, tk, IN_WIDTH), lambda l, k: (l, k, 0))],
        out_specs=pl.BlockSpec((1, tk, IN_WIDTH_PAD), lambda l, k: (l, k, 0)),
        out_shape=jax.ShapeDtypeStruct((depth, D_MODEL, IN_WIDTH_PAD), BF16),
        compiler_params=_cparams(("arbitrary", "arbitrary")),
        name="cast_pad_w_in",
    )(w_in)


def _inproj(layer, stream, rows, mods_l, norm_g, w_in_b, q_gain, k_gain, w2, b2, cos_t, sin_t, tm, n_ctx_tiles):
    row = lambda i: (i, 0)
    const = lambda i: (0, 0)
    xs, x_specs = _stream_specs(stream, tm, n_ctx_tiles)
    out_shapes = (
        jax.ShapeDtypeStruct((rows, ATTN_WIDTH), BF16),
        jax.ShapeDtypeStruct((rows, KV_WIDTH), BF16),
        jax.ShapeDtypeStruct((rows, KV_WIDTH), BF16),
        jax.ShapeDtypeStruct((rows, GLA_KEY_WIDTH), F32),
        jax.ShapeDtypeStruct((rows, GLA_KEY_WIDTH), F32),
        jax.ShapeDtypeStruct((rows, GLA_VALUE_WIDTH), BF16),
        jax.ShapeDtypeStruct((rows, GLA_VALUE_WIDTH), F32),
        jax.ShapeDtypeStruct((rows, 2 * GLA_KEY_WIDTH), F32),
    )
    return pl.pallas_call(
        functools.partial(_inproj_kernel, n_ctx_tiles),
        grid=(rows // tm,),
        in_specs=x_specs + [
            pl.BlockSpec(mods_l.shape, const),
            pl.BlockSpec((1, D_MODEL), const),
            pl.BlockSpec((1,) + w_in_b.shape[1:], lambda i: (layer, 0, 0), pipeline_mode=pl.Buffered(1)),
            pl.BlockSpec((1, HEAD_DIM), const),
            pl.BlockSpec((1, HEAD_DIM), const),
            pl.BlockSpec(w2.shape, const),
            pl.BlockSpec(b2.shape, const),
            pl.BlockSpec((tm, LANE), row),
            pl.BlockSpec((tm, LANE), row),
        ],
        out_specs=tuple(pl.BlockSpec((tm, s.shape[1]), row) for s in out_shapes),
        out_shape=out_shapes,
        compiler_params=_cparams(("arbitrary",)),
        name="inproj",
    )(*xs, mods_l, norm_g, w_in_b, q_gain, k_gain, w2, b2, cos_t, sin_t)


def _with_ones(v):
    return jnp.concatenate([v, jnp.ones_like(v)], axis=1)


def _attn_kernel(n_ctx_qtiles, n_ctx, kv_tile, n_pairs, q_ref, k_ref, v_ref, o_ref, s0_ref, s1_ref, m_ref, acc_ref):
    tq = q_ref.shape[0]
    q = jnp.concatenate([q_ref[:, g * HEAD_DIM:(g + 1) * HEAD_DIM] for g in range(ATTN_GROUP)], axis=0)
    is_ctx = pl.program_id(1) < n_ctx_qtiles

    def finish(acc):
        o = acc[:, :HEAD_DIM] / acc[:, HEAD_DIM:]
        o_ref[...] = jnp.concatenate([o[g * tq:(g + 1) * tq, :] for g in range(ATTN_GROUP)], axis=1).astype(BF16)

    @pl.when(is_ctx)
    def _():
        s = _dot_nt(q, k_ref[0:n_ctx, :])
        p = jnp.exp2(s - jnp.max(s, axis=1, keepdims=True)).astype(BF16)
        finish(_dot(p, _with_ones(v_ref[0:n_ctx, :])))

    @pl.when(jnp.logical_not(is_ctx))
    def _():
        reps = kv_tile // LANE
        m_ref[...] = jnp.full(m_ref.shape, -jnp.inf, F32)
        acc_ref[...] = jnp.zeros(acc_ref.shape, F32)

        def scores(tile):
            start = pl.multiple_of(tile * kv_tile, LANE)
            return _dot_nt(q, k_ref[pl.ds(start, kv_tile), :])

        def update(s_ref, tile):
            start = pl.multiple_of(tile * kv_tile, LANE)
            vext = _with_ones(v_ref[pl.ds(start, kv_tile), :])
            part = s_ref.shape[0] // ATTN_GROUP
            for rs in (slice(g * part, (g + 1) * part) for g in range(ATTN_GROUP)):
                s = s_ref[rs, :]
                m_prev = m_ref[rs, :]
                m_new = jnp.maximum(m_prev, jnp.max(s, axis=1, keepdims=True))
                alpha = jnp.exp2(m_prev - m_new)
                p = jnp.exp2(s - jnp.tile(m_new, (1, reps))).astype(BF16)
                acc_ref[rs, :] = jnp.tile(alpha, (1, 2)) * acc_ref[rs, :] + _dot(p, vext)
                m_ref[rs, :] = m_new

        s0_ref[...] = scores(0)

        def pair(jj, carry):
            s1_ref[...] = scores(2 * jj + 1)
            update(s0_ref, 2 * jj)
            s0_ref[...] = scores(jnp.minimum(2 * jj + 2, 2 * n_pairs - 1))
            update(s1_ref, 2 * jj + 1)
            return carry

        lax.fori_loop(0, n_pairs, pair, 0)
        finish(acc_ref[...])


def _attention(q, k, v, n_ctx, tq, kv_tile):
    rows = q.shape[0]
    gw = ATTN_GROUP * HEAD_DIM
    n_tiles = rows // kv_tile
    score = pltpu.VMEM((ATTN_GROUP * tq, kv_tile), F32)
    return pl.pallas_call(
        functools.partial(_attn_kernel, n_ctx // tq, n_ctx, kv_tile, n_tiles // 2),
        grid=(ATTN_KV_HEADS, rows // tq),
        in_specs=[
            pl.BlockSpec((tq, gw), lambda g, i: (i, g)),
            pl.BlockSpec((rows, HEAD_DIM), lambda g, i: (0, g)),
            pl.BlockSpec((rows, HEAD_DIM), lambda g, i: (0, g)),
        ],
        out_specs=pl.BlockSpec((tq, gw), lambda g, i: (i, g)),
        out_shape=jax.ShapeDtypeStruct((rows, ATTN_WIDTH), BF16),
        scratch_shapes=[score, score, pltpu.VMEM((ATTN_GROUP * tq, LANE), F32),
                        pltpu.VMEM((ATTN_GROUP * tq, 2 * HEAD_DIM), F32)],
        compiler_params=_cparams(("arbitrary", "arbitrary")),
        name="attention",
    )(q, k, v)


def _kv_tile(rows):
    for n in range(1536 // LANE, 0, -1):
        if rows % (n * LANE) == 0 and (rows // (n * LANE)) % 2 == 0:
            return n * LANE
    raise ValueError("unsupported token count for the attention key tiling")


def _gla_direction(reverse, q_ref, k_ref, v_ref, la_ref, o_ref, s_ref):
    n = GLA_CHUNK
    row = lax.broadcasted_iota(jnp.int32, (n, n), 0)
    col = lax.broadcasted_iota(jnp.int32, (n, n), 1)
    tri = jnp.where((col >= row) if reverse else (col <= row), 1.0, 0.0).astype(BF16)
    valid_pair = (col >= row) if reverse else (col <= row)
    rows1 = lax.broadcasted_iota(jnp.int32, (n, 1), 0)
    last = 0 if reverse else n - 1
    n_sub = q_ref.shape[0] // n
    b_all = {}
    for c in range(n_sub):
        pieces = _split3(la_ref[c * n:(c + 1) * n, :])
        b_all[c] = _dot(tri, pieces[0]) + _dot(tri, pieces[1]) + _dot(tri, pieces[2])
    for c, h in [(c, h) for c in (range(n_sub - 1, -1, -1) if reverse else range(n_sub)) for h in range(GLA_HEADS)]:
        rs = slice(c * n, (c + 1) * n)
        ks = slice(h * GLA_DK, (h + 1) * GLA_DK)
        q = q_ref[rs, ks]
        k = k_ref[rs, ks]
        v = v_ref[rs, h * GLA_DV:(h + 1) * GLA_DV]
        b = b_all[c][:, ks]
        b_tot = b[last:last + 1, :]
        a_rows = []
        for blk in range(n // GLA_SUB):
            lo, hi = blk * GLA_SUB, (blk + 1) * GLA_SUB
            ref_row = hi - 1 if reverse else lo
            b_ref = b[ref_row:ref_row + 1, :]
            q_blk = q[lo:hi, :] * jnp.exp(b[lo:hi, :] - b_ref)
            reach = (rows1 >= lo) if reverse else (rows1 < hi)
            k_blk = jnp.where(reach, k * jnp.exp(jnp.where(reach, b_ref - b, 0.0)), 0.0)
            a_rows.append(_dot_nt(q_blk.astype(BF16), k_blk.astype(BF16)))
        a = jnp.where(valid_pair, jnp.concatenate(a_rows, axis=0), 0.0).astype(BF16)
        st = s_ref[h]
        o = _dot(a, v) + _dot_nt((q * jnp.exp(b)).astype(BF16), st.astype(BF16))
        o_ref[rs, h * GLA_DV:(h + 1) * GLA_DV] = o
        k_end = (k * jnp.exp(b_tot - b)).astype(BF16)
        s_ref[h] = jnp.exp(b_tot) * st + _dot_tn(v, k_end)


def _gla_kernel(qf, kf, vf, laf, qb, kb, vb, lab, of_ref, ob_ref, sf_ref, sb_ref):
    @pl.when(pl.program_id(0) == 0)
    def _():
        sf_ref[...] = jnp.zeros(sf_ref.shape, F32)
        sb_ref[...] = jnp.zeros(sb_ref.shape, F32)

    _gla_direction(False, qf, kf, vf, laf, of_ref, sf_ref)
    _gla_direction(True, qb, kb, vb, lab, ob_ref, sb_ref)


def _gla(gq, gk, gv, la, n_ctx):
    rows = gq.shape[0]
    step = next(m * GLA_CHUNK for m in (4, 2, 1) if n_ctx % (m * GLA_CHUNK) == 0 and rows % (m * GLA_CHUNK) == 0)
    n_chunks = rows // step
    n_ctx_chunks = n_ctx // step

    def bchunk(s):
        return jnp.where(s < n_ctx_chunks, n_ctx_chunks - 1 - s, n_chunks - 1 - s + n_ctx_chunks)

    fwd = lambda s: (s, 0)
    bwd = lambda s: (bchunk(s), 0)
    bwd_la = lambda s: (bchunk(s), 1)
    key_blk = (step, GLA_KEY_WIDTH)
    val_blk = (step, GLA_VALUE_WIDTH)
    state = pltpu.VMEM((GLA_HEADS, GLA_DV, GLA_DK), F32)
    return pl.pallas_call(
        _gla_kernel,
        grid=(n_chunks,),
        in_specs=[
            pl.BlockSpec(key_blk, fwd), pl.BlockSpec(key_blk, fwd), pl.BlockSpec(val_blk, fwd),
            pl.BlockSpec(key_blk, fwd),
            pl.BlockSpec(key_blk, bwd), pl.BlockSpec(key_blk, bwd), pl.BlockSpec(val_blk, bwd),
            pl.BlockSpec(key_blk, bwd_la),
        ],
        out_specs=(pl.BlockSpec(val_blk, fwd), pl.BlockSpec(val_blk, bwd)),
        out_shape=(jax.ShapeDtypeStruct((rows, GLA_VALUE_WIDTH), F32),) * 2,
        scratch_shapes=[state, state],
        compiler_params=_cparams(("arbitrary",)),
        name="gla_scan",
    )(gq, gk, gv, la, gq, gk, gv, la)


def _outproj_kernel(n_ctx_tiles, xc_ref, xl_ref, a_ref, of_ref, ob_ref, r_ref, m_ref, gg_ref, w_ref, nf_ref,
                    wrh_ref, wrl_ref, xo_ref, h2_ref, aff_ref, wb_ref):
    is_ctx = pl.program_id(0) < n_ctx_tiles

    @pl.when(pl.program_id(0) == 0)
    def _():
        wb_ref[...] = w_ref[0].astype(BF16)

    gla = of_ref[...] + ob_ref[...]
    r = r_ref[...]
    parts = [a_ref[...]]
    for h in range(GLA_HEADS):
        sl = slice(h * GLA_DV, (h + 1) * GLA_DV)
        parts.append((_rms(gla[:, sl], gg_ref[...]) * _silu(r[:, sl])).astype(BF16))
    y = _dot(jnp.concatenate(parts, axis=1), wb_ref[...])
    x = jnp.where(is_ctx, xc_ref[...], xl_ref[...]) + _mod_rows(m_ref, is_ctx, 2) * y
    xo_ref[...] = x
    h2 = _rms(x, nf_ref[...]) * (1.0 + _mod_rows(m_ref, is_ctx, 4)) + _mod_rows(m_ref, is_ctx, 3)
    h2_ref[...] = h2
    h_hi, h_lo = _split2(h2)
    logits = _dot(h_hi, wrh_ref[...]) + _dot(h_lo, wrh_ref[...]) + _dot(h_hi, wrl_ref[...])
    live = lax.broadcasted_iota(jnp.int32, logits.shape, 1) < N_EXPERTS
    logits = jnp.where(live, logits, -jnp.inf)
    e = jnp.exp(logits - jnp.max(logits, axis=-1, keepdims=True))
    aff_ref[...] = e / jnp.sum(e, axis=-1, keepdims=True)


def _outproj(layer, stream, rows, attn_o, o_f, o_b, r, mods_l, gla_gain, w_out, norm_ffn, wr_hi, wr_lo, tm,
             n_ctx_tiles):
    row = lambda i: (i, 0)
    const = lambda i: (0, 0)
    xs, x_specs = _stream_specs(stream, tm, n_ctx_tiles)
    return pl.pallas_call(
        functools.partial(_outproj_kernel, n_ctx_tiles),
        grid=(rows // tm,),
        in_specs=x_specs + [
            pl.BlockSpec((tm, ATTN_WIDTH), row),
            pl.BlockSpec((tm, GLA_VALUE_WIDTH), row),
            pl.BlockSpec((tm, GLA_VALUE_WIDTH), row),
            pl.BlockSpec((tm, GLA_VALUE_WIDTH), row),
            pl.BlockSpec(mods_l.shape, const),
            pl.BlockSpec((1, GLA_DV), const),
            pl.BlockSpec((1,) + w_out.shape[1:], lambda i: (layer, 0, 0), pipeline_mode=pl.Buffered(1)),
            pl.BlockSpec((1, D_MODEL), const),
            pl.BlockSpec(wr_hi.shape, const),
            pl.BlockSpec(wr_lo.shape, const),
        ],
        out_specs=(
            pl.BlockSpec((tm, D_MODEL), row),
            pl.BlockSpec((tm, D_MODEL), row),
            pl.BlockSpec((tm, LANE), row),
        ),
        out_shape=(
            jax.ShapeDtypeStruct((rows, D_MODEL), F32),
            jax.ShapeDtypeStruct((rows, D_MODEL), F32),
            jax.ShapeDtypeStruct((rows, LANE), F32),
        ),
        scratch_shapes=[pltpu.VMEM(w_out.shape[1:], BF16)],
        compiler_params=_cparams(("arbitrary",)),
        name="outproj",
    )(*xs, attn_o, o_f, o_b, r, mods_l, gla_gain, w_out, norm_ffn, wr_hi, wr_lo)


def _exclusive_rank(mask_b):
    nr = mask_b.shape[0]
    li = lax.broadcasted_iota(jnp.int32, (LANE, LANE), 0)
    lj = lax.broadcasted_iota(jnp.int32, (LANE, LANE), 1)
    within = _dot(mask_b, jnp.where(li < lj, 1.0, 0.0).astype(BF16))
    ri = lax.broadcasted_iota(jnp.int32, (nr, nr), 0)
    rj = lax.broadcasted_iota(jnp.int32, (nr, nr), 1)
    before = jnp.sum(_dot(jnp.where(rj < ri, 1.0, 0.0).astype(BF16), mask_b), axis=1, keepdims=True)
    return within + before, before


SEARCH_STEPS = 127
COMPACT_WINDOW = LANE + 16


def _select_kernel(cap, aff_ref, pos_ref, off_ref):
    a = aff_ref[...]
    ne = a.shape[0]

    def count(mask):
        return jnp.sum(jnp.sum(jnp.where(mask, 1.0, 0.0), axis=2, keepdims=True), axis=1, keepdims=True)

    def search(_, carry):
        t, step = carry
        cand = t + step
        return jnp.where(count(a >= cand) >= cap, cand, t), step * 0.5

    thr, _ = lax.fori_loop(0, SEARCH_STEPS, search, (jnp.zeros((ne, 1, 1), F32), jnp.ones((1, 1, 1), F32)))
    room = cap - count(a > thr)
    for e in range(ne):
        above = a[e] > thr[e]
        tie = a[e] == thr[e]
        tie_rank, _ = _exclusive_rank(jnp.where(tie, 1.0, 0.0).astype(BF16))
        chosen = above | (tie & (tie_rank < room[e]))
        slot, before = _exclusive_rank(jnp.where(chosen, 1.0, 0.0).astype(BF16))
        pos_ref[e] = jnp.where(chosen, slot, -1.0)
        off_ref[e] = jnp.broadcast_to(before, slot.shape)


def _compact_kernel(cap, off_ref, pos_ref, aff_ref, out_ref, acc_ref):
    e = pl.program_id(0)
    nr = pos_ref.shape[1]
    acc_ref[...] = jnp.zeros(acc_ref.shape, F32)
    rel_ids = lax.broadcasted_iota(jnp.int32, (COMPACT_WINDOW, LANE), 0).astype(F32)
    lane_row = lax.broadcasted_iota(jnp.int32, (1, LANE), 1).astype(F32)
    pad_rows = jnp.zeros((LANE - SUBLANE, LANE), BF16)

    def gather_row(r, carry):
        start = pl.multiple_of(jnp.minimum((off_ref[e, r] // SUBLANE) * SUBLANE, cap), SUBLANE)
        slot_ids = rel_ids + start.astype(F32)
        onehot = jnp.where(slot_ids == pos_ref[0, pl.ds(r, 1), :], 1.0, 0.0).astype(BF16)
        a_hi, a_mid, a_lo = _split3(aff_ref[0, pl.ds(r, 1), :])
        r_row = jnp.full((1, LANE), r, jnp.int32).astype(F32)
        vals = jnp.concatenate(
            [lane_row.astype(BF16), r_row.astype(BF16), a_hi, a_mid, a_lo, jnp.zeros((3, LANE), BF16), pad_rows],
            axis=0)
        acc_ref[pl.ds(start, COMPACT_WINDOW), :] += _dot_nt(onehot, vals)
        return carry

    lax.fori_loop(0, nr, gather_row, 0, unroll=8)
    acc = acc_ref[0:cap, :]
    token = acc[:, 0:1] + LANE * acc[:, 1:2]
    gate = acc[:, 2:3] + acc[:, 3:4] + acc[:, 4:5]
    lane = lax.broadcasted_iota(jnp.int32, (cap, LANE), 1)
    out_ref[0] = jnp.where(lane == 0, token, jnp.where(lane == 1, gate, 0.0))


def _route(aff_t, cap):
    ne, nr, _ = aff_t.shape
    assert nr <= 256 and nr % SUBLANE == 0 and cap % SUBLANE == 0
    whole = pl.BlockSpec((ne, nr, LANE), lambda i: (0, 0, 0))
    pos, off = pl.pallas_call(
        functools.partial(_select_kernel, cap),
        grid=(1,),
        in_specs=[whole],
        out_specs=(whole, whole),
        out_shape=(jax.ShapeDtypeStruct(aff_t.shape, F32),) * 2,
        compiler_params=_cparams(("arbitrary",)),
        name="route_select",
    )(aff_t)
    per_expert = pl.BlockSpec((1, nr, LANE), lambda e, off: (e, 0, 0))
    return pl.pallas_call(
        functools.partial(_compact_kernel, cap),
        grid_spec=pltpu.PrefetchScalarGridSpec(
            num_scalar_prefetch=1,
            grid=(ne,),
            in_specs=[per_expert, per_expert],
            out_specs=pl.BlockSpec((1, cap, LANE), lambda e, off: (e, 0, 0)),
            scratch_shapes=[pltpu.VMEM((cap + COMPACT_WINDOW, LANE), F32)],
        ),
        out_shape=jax.ShapeDtypeStruct((ne, cap, LANE), F32),
        compiler_params=_cparams(("arbitrary",)),
        name="route_compact",
    )(off[:, :, 0].astype(jnp.int32), pos, aff_t)


def _route_tokens(aff, lo, n, offset):
    cap = CAPACITY_FACTOR * n // N_EXPERTS
    a = aff[lo:lo + n, :N_EXPERTS].T.reshape(N_EXPERTS, n // LANE, LANE)
    pad = (-a.shape[1]) % SUBLANE
    if pad:
        a = jnp.concatenate([a, jnp.full((N_EXPERTS, pad, LANE), -1.0, F32)], axis=1)
    sel = _route(a, cap)
    return sel[:, :, 0].astype(jnp.int32) + offset, sel[:, :, 1]


def _moe_kernel(n_lat, nf, idx_ref, gate_ref, m_ref, wg_ref, wu_ref, wd_ref, h2_hbm, x_hbm, xo_hbm,
                stage_h, stage_x, xg, acc, sem_h, sem_in, sem_out):
    del x_hbm
    e = pl.program_id(0)
    f = pl.program_id(1)
    ne = pl.num_programs(0)
    groups = stage_h.shape[0]
    ns = groups * SUBLANE
    per_step = ns // nf
    down_cols = 4 * LANE

    def row_copy(t, g, i, to_vmem, hbm, stage, sem):
        if to_vmem:
            return pltpu.make_async_copy(hbm.at[pl.ds(t, 1), :], stage.at[g, pl.ds(i, 1), :], sem)
        return pltpu.make_async_copy(stage.at[g, pl.ds(i, 1), :], hbm.at[pl.ds(t, 1), :], sem)

    def start_all_rows(expert, to_vmem, hbm, stage, sem):
        def group(g, carry):
            first = expert * ns + g * SUBLANE
            for i in range(SUBLANE):
                row_copy(idx_ref[first + i], g, i, to_vmem, hbm, stage, sem).start()
            return carry
        lax.fori_loop(0, groups, group, 0)

    def start_step_rows(expert, to_vmem, hbm, stage, sem):
        first = expert * ns + f * per_step
        for j in range(per_step):
            if per_step % SUBLANE == 0:
                g, i = f * (per_step // SUBLANE) + j // SUBLANE, j % SUBLANE
            else:
                s = f * per_step + j
                g, i = s // SUBLANE, s % SUBLANE
            row_copy(idx_ref[first + j], g, i, to_vmem, hbm, stage, sem).start()

    def wait_rows(stage, sem):
        pltpu.make_async_copy(stage, stage, sem).wait()

    @pl.when(f == 0)
    def _():
        @pl.when(e == 0)
        def _():
            start_all_rows(0, True, h2_hbm, stage_h, sem_h)

        wait_rows(stage_h, sem_h)
        xg[...] = stage_h[...].reshape(ns, D_MODEL).astype(BF16)
        acc[...] = jnp.zeros(acc.shape, F32)

    @pl.when(f == (nf - 3 if nf > 4 else nf - 2))
    def _():
        @pl.when(e > 0)
        def _():
            wait_rows(stage_x, sem_out)

        start_all_rows(e, True, xo_hbm, stage_x, sem_in)

    start_step_rows(jnp.minimum(e + 1, ne - 1), True, h2_hbm, stage_h, sem_h)
    x = xg[...]
    hid = (_silu(_dot(x, wg_ref[0, 0].astype(BF16))) * _dot(x, wu_ref[0, 0].astype(BF16))).astype(BF16)
    for n in range(D_MODEL // down_cols):
        cols = slice(n * down_cols, (n + 1) * down_cols)
        part = _dot(hid, wd_ref[0, 0, :, cols].astype(BF16))
        acc[:, cols] += part

    @pl.when(f == nf - 1)
    def _():
        is_lat = lax.broadcasted_iota(jnp.int32, (ns, 1), 0) < n_lat
        gate = gate_ref[0]
        wait_rows(stage_x, sem_in)
        for n in range(D_MODEL // down_cols):
            cols = slice(n * down_cols, (n + 1) * down_cols)
            mcols = slice(5 * D_MODEL + n * down_cols, 5 * D_MODEL + (n + 1) * down_cols)
            scale = gate * jnp.where(is_lat, m_ref[0:1, mcols], m_ref[1:2, mcols])
            y = (acc[:, cols] * scale).reshape(groups, SUBLANE, down_cols)
            stage_x[:, :, cols] = stage_x[:, :, cols] + y
        start_all_rows(e, False, xo_hbm, stage_x, sem_out)

        @pl.when(e == ne - 1)
        def _():
            wait_rows(stage_x, sem_out)
            wait_rows(stage_h, sem_h)


def _moe(layer, idx, gate, mods_l, w_gate, w_up, w_down, h2, x2, n_lat):
    ne, ns = idx.shape
    ft = 256 if ns % (EXPERT_FF // 256) == 0 else 384
    nf = EXPERT_FF // ft
    assert nf >= 2 and ns % nf == 0 and ns % SUBLANE == 0
    stage = pltpu.VMEM((ns // SUBLANE, SUBLANE, D_MODEL), F32)
    grid_spec = pltpu.PrefetchScalarGridSpec(
        num_scalar_prefetch=1,
        grid=(ne, nf),
        in_specs=[
            pl.BlockSpec((1, ns, 1), lambda e, f, idx: (e, 0, 0)),
            pl.BlockSpec(mods_l.shape, lambda e, f, idx: (0, 0)),
            pl.BlockSpec((1, 1, D_MODEL, ft), lambda e, f, idx: (layer, e, 0, f)),
            pl.BlockSpec((1, 1, D_MODEL, ft), lambda e, f, idx: (layer, e, 0, f)),
            pl.BlockSpec((1, 1, ft, D_MODEL), lambda e, f, idx: (layer, e, f, 0)),
            pl.BlockSpec(memory_space=pl.ANY),
            pl.BlockSpec(memory_space=pl.ANY),
        ],
        out_specs=pl.BlockSpec(memory_space=pl.ANY),
        scratch_shapes=[
            stage,
            stage,
            pltpu.VMEM((ns, D_MODEL), BF16),
            pltpu.VMEM((ns, D_MODEL), F32),
            pltpu.SemaphoreType.DMA,
            pltpu.SemaphoreType.DMA,
            pltpu.SemaphoreType.DMA,
        ],
    )
    return pl.pallas_call(
        functools.partial(_moe_kernel, n_lat, nf),
        grid_spec=grid_spec,
        out_shape=jax.ShapeDtypeStruct(x2.shape, F32),
        input_output_aliases={7: 0},
        compiler_params=_cparams(("arbitrary", "arbitrary")),
        name="moe",
    )(idx.reshape(-1), gate[:, :, None], mods_l, w_gate, w_up, w_down, h2, x2)


def _final_kernel(x_ref, g_ref, o_ref):
    o_ref[...] = _rms(x_ref[...], g_ref[...])


def _final_norm(x2, gain, n_ctx, tm):
    n_lat = x2.shape[0] - n_ctx
    skip = n_ctx // tm
    return pl.pallas_call(
        _final_kernel,
        grid=(n_lat // tm,),
        in_specs=[
            pl.BlockSpec((tm, D_MODEL), lambda i: (i + skip, 0)),
            pl.BlockSpec((1, D_MODEL), lambda i: (0, 0)),
        ],
        out_specs=pl.BlockSpec((tm, D_MODEL), lambda i: (i, 0)),
        out_shape=jax.ShapeDtypeStruct((n_lat, D_MODEL), F32),
        compiler_params=_cparams(("arbitrary",)),
        name="final_norm",
    )(x2, gain)


def _rope_tables(n_lat, n_ctx):
    rows = n_lat // GRID_W
    row = jnp.repeat(jnp.arange(rows, dtype=F32), GRID_W)
    col = jnp.tile(jnp.arange(GRID_W, dtype=F32), rows)
    half = HEAD_DIM // 2
    inv_freq = ROPE_THETA ** (-jnp.arange(0, half, 2, dtype=F32) / half)
    ang = jnp.concatenate([row[:, None] * inv_freq, col[:, None] * inv_freq], axis=-1)
    cos = jnp.repeat(jnp.cos(ang), 2, axis=-1)
    sign = jnp.tile(jnp.array([-1.0, 1.0], F32), half)
    sin = jnp.repeat(jnp.sin(ang), 2, axis=-1) * sign
    cos = jnp.concatenate([jnp.ones((n_ctx, HEAD_DIM), F32), cos], axis=0)
    sin = jnp.concatenate([jnp.zeros((n_ctx, HEAD_DIM), F32), sin], axis=0)
    return cos, sin


def _row_tile(n_lat, n_ctx):
    for tm in (256, 128, 64):
        if n_lat % tm == 0 and n_ctx % tm == 0:
            return tm
    raise ValueError("token counts must be multiples of 64")


def kernel(x, c, ctx, c_ctx, w_mod, b_mod, norm_mix, w_in, q_gain, k_gain, w_gla_a2, b_gla_a, gla_gain,
           w_out, norm_ffn, w_router, w_gate, w_up, w_down, final_norm):
    batch, n_lat, _ = x.shape
    n_ctx = ctx.shape[1]
    depth = w_mod.shape[0]
    assert batch == 1 and c.shape[0] == 1
    tm = _row_tile(n_lat, n_ctx)
    n_ctx_tiles = n_ctx // tm
    tq = next(t for t in (256, 128, 64) if n_ctx % t == 0)
    kv_tile = _kv_tile(n_ctx + n_lat)

    cvec = jnp.concatenate([c, c_ctx[None, :], jnp.zeros((SUBLANE - 2, D_MODEL), F32)], axis=0)
    mods = _modulation(cvec, w_mod, b_mod)
    cos_t, sin_t = _rope_tables(n_lat, n_ctx)
    rows = n_ctx + n_lat
    stream = (ctx[0], x[0], 0)

    w_in_b = _cast_pad_in_weights(w_in)
    for l in range(depth):
        last = l == depth - 1
        w2 = jnp.zeros((LANE, 2 * GLA_KEY_WIDTH), F32)
        o0 = IN_WIDTH - 2 * GLA_GATE_RANK - GATE_COL
        w2 = w2.at[o0:o0 + GLA_GATE_RANK, :GLA_KEY_WIDTH].set(w_gla_a2[l, 0])
        w2 = w2.at[o0 + GLA_GATE_RANK:o0 + 2 * GLA_GATE_RANK, GLA_KEY_WIDTH:].set(w_gla_a2[l, 1])
        b2 = b_gla_a[l].reshape(1, 2 * GLA_KEY_WIDTH)
        q, k, v, gq, gk, gv, r, la = _inproj(
            l, stream, rows, mods[l], norm_mix[l][None, :], w_in_b, q_gain[l][None, :], k_gain[l][None, :], w2, b2,
            cos_t, sin_t, tm, n_ctx_tiles)
        attn_o = _attention(q, k, v, n_ctx, tq, kv_tile)
        o_f, o_b = _gla(gq, gk, gv, la, n_ctx)
        wr = jnp.pad(w_router[l], ((0, 0), (0, LANE - N_EXPERTS)))
        wr_hi, wr_lo = _split2(wr)
        x2, h2, aff = _outproj(l, stream, rows, attn_o, o_f, o_b, r, mods[l], gla_gain[l][None, :],
                               w_out, norm_ffn[l][None, :], wr_hi, wr_lo, tm, n_ctx_tiles)
        idx, gate = _route_tokens(aff, n_ctx, n_lat, n_ctx)
        n_lat_slots = idx.shape[1]
        if not last:
            idx_c, gate_c = _route_tokens(aff, 0, n_ctx, 0)
            idx = jnp.concatenate([idx, idx_c], axis=1)
            gate = jnp.concatenate([gate, gate_c], axis=1)
        x2 = _moe(l, idx, gate, mods[l], w_gate, w_up, w_down, h2, x2, n_lat_slots)
        stream = (x2, x2, n_ctx_tiles)

    return _final_norm(x2, final_norm[None, :], n_ctx, tm)[None]
```
